```python
import jax, jax.numpy as jnp
from jax import lax
import numpy as np

D_MODEL = 2048
BATCH = 8
SEQ = 2048
DEPTH = 1

HEAD_DIM = 64
N_Q_HEADS = 32
N_KV_HEADS = 4
GROUP = N_Q_HEADS // N_KV_HEADS
WINDOW = 128
BLOCK = 128
D_ATTN = N_Q_HEADS * HEAD_DIM
D_KV = N_KV_HEADS * HEAD_DIM
D_RNN = 2560
N_RNN_BLOCKS = 16
RNN_BLOCK_W = D_RNN // N_RNN_BLOCKS
RNN_CONV_W = 4
LRU_C = 8.0
D_FF = 3 * D_MODEL
FFN_CONV_W = 3
LN_EPS = 1e-5
ALPHA = (2 * DEPTH) ** 0.25
BETA = (8 * DEPTH) ** -0.25
SPLIT_SIZES = (D_ATTN, D_KV, D_KV, D_RNN, D_RNN, 2 * D_MODEL)
SPLIT_POINTS = tuple(int(s) for s in np.cumsum(SPLIT_SIZES)[:-1])
D_IN = int(sum(SPLIT_SIZES))
V_START = D_ATTN + D_KV

kernel_name = "hybrid_swa_rglru_gated_merge_deepnorm"


def alibi_slopes(n_heads):
    h = np.arange(1, n_heads + 1, dtype=np.float32)
    return jnp.asarray(2.0 ** (-8.0 * h / n_heads), dtype=jnp.float32)


def layer_norm(x, g, b):
    x32 = x.astype(jnp.float32)
    mu = jnp.mean(x32, axis=-1, keepdims=True)
    var = jnp.mean(jnp.square(x32 - mu), axis=-1, keepdims=True)
    y = (x32 - mu) * lax.rsqrt(var + LN_EPS) * g.astype(jnp.float32) + b.astype(jnp.float32)
    return y.astype(x.dtype)


def causal_dwconv(x, w, b):
    width = w.shape[0]
    s = x.shape[1]
    xp = jnp.pad(x, ((0, 0), (width - 1, 0), (0, 0)))
    y = sum(xp[:, k:k + s] * w[k] for k in range(width))
    return y + b


def sliding_window_attention(q, k, v, sinks):
    bsz, s = q.shape[0], q.shape[1]
    nb = s // BLOCK
    qb = q.reshape(bsz, nb, BLOCK, N_KV_HEADS, GROUP, HEAD_DIM)

    def band(t):
        tb = t.reshape(bsz, nb, BLOCK, N_KV_HEADS, HEAD_DIM)
        prev = jnp.pad(tb, ((0, 0), (1, 0), (0, 0), (0, 0), (0, 0)))[:, :-1]
        return jnp.concatenate([prev, tb], axis=2)

    kb, vb = band(k), band(v)
    scores = jnp.einsum('bnqhgd,bnkhd->bhgnqk', qb, kb).astype(jnp.float32) * (HEAD_DIM ** -0.5)
    qi = jnp.arange(BLOCK)[:, None]
    kj = jnp.arange(2 * BLOCK)[None, :]
    dist = BLOCK + qi - kj
    blk = jnp.arange(nb)[:, None, None]
    key_pos = (blk - 1) * BLOCK + kj
    valid = (dist >= 0) & (dist < WINDOW) & (key_pos >= 0)
    slopes = alibi_slopes(N_Q_HEADS).reshape(N_KV_HEADS, GROUP, 1, 1, 1)
    scores = scores - slopes * dist.astype(jnp.float32)
    scores = jnp.where(valid, scores, -jnp.inf)
    sink = sinks.astype(jnp.float32).reshape(N_KV_HEADS, GROUP, 1, 1, 1)
    m = jnp.maximum(jnp.max(scores, axis=-1, keepdims=True), sink)
    p = jnp.exp(scores - m)
    denom = jnp.sum(p, axis=-1, keepdims=True) + jnp.exp(sink - m)
    p = (p / denom).astype(v.dtype)
    o = jnp.einsum('bhgnqk,bnkhd->bnqhgd', p, vb)
    return o.reshape(bsz, s, D_ATTN)


def rg_lru(x, w_a, b_a, w_i, b_i, lam):
    bsz, s, _ = x.shape
    xb = x.reshape(bsz, s, N_RNN_BLOCKS, RNN_BLOCK_W)
    r = jax.nn.sigmoid(jnp.einsum('bsnc,ncd->bsnd', xb, w_a).reshape(bsz, s, D_RNN) + b_a)
    i = jax.nn.sigmoid(jnp.einsum('bsnc,ncd->bsnd', xb, w_i).reshape(bsz, s, D_RNN) + b_i)
    log_a = -LRU_C * r.astype(jnp.float32) * jax.nn.softplus(-lam.astype(jnp.float32))
    a = jnp.exp(log_a)
    u = jnp.sqrt(-jnp.expm1(2.0 * log_a)) * (i * x).astype(jnp.float32)

    def combine(left, right):
        a_l, b_l = left
        a_r, b_r = right
        return a_l * a_r, a_r * b_l + b_r

    _, h = lax.associative_scan(combine, (a, u), axis=1)
    return h.astype(x.dtype)


def setup_inputs(seed: int = 0) -> dict:
    key = jax.random.key(seed)
    ks = jax.random.split(key, 24)
    f32 = jnp.float32
    L = DEPTH
    nrm = lambda k, shape, scale: jax.random.normal(k, shape, f32) * scale
    x = jax.random.normal(ks[0], (BATCH, SEQ, D_MODEL), f32)
    w_in = nrm(ks[1], (L, D_MODEL, D_IN), D_MODEL ** -0.5)
    w_in = w_in.at[:, :, V_START:V_START + D_KV].multiply(BETA)
    b_gate = nrm(ks[2], (L, 2 * D_MODEL), 0.1)
    rnn_conv_w = nrm(ks[3], (L, RNN_CONV_W, D_RNN), RNN_CONV_W ** -0.5)
    rnn_conv_b = nrm(ks[4], (L, D_RNN), 0.02)
    lru_wa = nrm(ks[5], (L, N_RNN_BLOCKS, RNN_BLOCK_W, RNN_BLOCK_W), RNN_BLOCK_W ** -0.5)
    lru_ba = nrm(ks[6], (L, D_RNN), 0.02)
    lru_wi = nrm(ks[7], (L, N_RNN_BLOCKS, RNN_BLOCK_W, RNN_BLOCK_W), RNN_BLOCK_W ** -0.5)
    lru_bi = nrm(ks[8], (L, D_RNN), 0.02)
    a_c = jax.random.uniform(ks[9], (L, D_RNN), f32, minval=0.9, maxval=0.999)
    a0 = a_c ** (1.0 / LRU_C)
    lru_lambda = jnp.log(a0) - jnp.log1p(-a0)
    attn_sinks = nrm(ks[10], (L, N_Q_HEADS), 0.5)
    w_attn_proj = nrm(ks[11], (L, D_ATTN, D_MODEL), BETA * D_ATTN ** -0.5)
    w_rnn_proj = nrm(ks[12], (L, D_RNN, D_MODEL), BETA * D_RNN ** -0.5)
    w_out = nrm(ks[13], (L, D_MODEL, D_MODEL), BETA * D_MODEL ** -0.5)
    ln1_g = 1.0 + nrm(ks[14], (L, D_MODEL), 0.02)
    ln1_b = nrm(ks[15], (L, D_MODEL), 0.02)
    ffn_w_up = nrm(ks[16], (L, D_MODEL, D_FF), D_MODEL ** -0.5)
    ffn_w_gate = nrm(ks[17], (L, D_MODEL, D_FF), D_MODEL ** -0.5)
    ffn_conv_w = nrm(ks[18], (L, FFN_CONV_W, D_FF), FFN_CONV_W ** -0.5)
    ffn_conv_b = nrm(ks[19], (L, D_FF), 0.02)
    ffn_w_down = nrm(ks[20], (L, D_FF, D_MODEL), BETA * D_FF ** -0.5)
    ln2_g = 1.0 + nrm(ks[21], (L, D_MODEL), 0.02)
    ln2_b = nrm(ks[22], (L, D_MODEL), 0.02)
    return {"x": x, "w_in": w_in, "b_gate": b_gate, "rnn_conv_w": rnn_conv_w,
            "rnn_conv_b": rnn_conv_b, "lru_wa": lru_wa, "lru_ba": lru_ba, "lru_wi": lru_wi,
            "lru_bi": lru_bi, "lru_lambda": lru_lambda, "attn_sinks": attn_sinks,
            "w_attn_proj": w_attn_proj, "w_rnn_proj": w_rnn_proj, "w_out": w_out,
            "ln1_g": ln1_g, "ln1_b": ln1_b, "ffn_w_up": ffn_w_up, "ffn_w_gate": ffn_w_gate,
            "ffn_conv_w": ffn_conv_w, "ffn_conv_b": ffn_conv_b, "ffn_w_down": ffn_w_down,
            "ln2_g": ln2_g, "ln2_b": ln2_b}


def reference(x, w_in, b_gate, rnn_conv_w, rnn_conv_b, lru_wa, lru_ba, lru_wi, lru_bi,
              lru_lambda, attn_sinks, w_attn_proj, w_rnn_proj, w_out, ln1_g, ln1_b,
              ffn_w_up, ffn_w_gate, ffn_conv_w, ffn_conv_b, ffn_w_down, ln2_g, ln2_b):
    bsz, s, _ = x.shape
    for l in range(DEPTH):
        proj = x @ w_in[l]
        q, k, v, rx, ry, gl = jnp.split(proj, SPLIT_POINTS, axis=-1)
        q = q.reshape(bsz, s, N_Q_HEADS, HEAD_DIM)
        k = k.reshape(bsz, s, N_KV_HEADS, HEAD_DIM)
        v = v.reshape(bsz, s, N_KV_HEADS, HEAD_DIM)
        y_attn = sliding_window_attention(q, k, v, attn_sinks[l]) @ w_attn_proj[l]
        rx = causal_dwconv(rx, rnn_conv_w[l], rnn_conv_b[l])
        hr = rg_lru(rx, lru_wa[l], lru_ba[l], lru_wi[l], lru_bi[l], lru_lambda[l])
        y_rnn = (hr * jax.nn.gelu(ry, approximate=True)) @ w_rnn_proj[l]
        g_attn, g_rnn = jnp.split(jax.nn.sigmoid(gl + b_gate[l]), 2, axis=-1)
        mix = (g_attn * y_attn + g_rnn * y_rnn) @ w_out[l]
        x = layer_norm(ALPHA * x + mix, ln1_g[l], ln1_b[l])
        up = x @ ffn_w_up[l]
        gate = causal_dwconv(x @ ffn_w_gate[l], ffn_conv_w[l], ffn_conv_b[l])
        f = (jax.nn.gelu(gate, approximate=True) * up) @ ffn_w_down[l]
        x = layer_norm(ALPHA * x + f, ln2_g[l], ln2_b[l])
    return x
```

```python
import functools

import numpy as np
import jax
import jax.numpy as jnp
from jax import lax
from jax.experimental import pallas as pl
from jax.experimental.pallas import tpu as pltpu

F32 = jnp.float32
BF16 = jnp.bfloat16

HEAD_DIM = 64
N_Q_HEADS = 32
N_KV_HEADS = 4
GROUP = N_Q_HEADS // N_KV_HEADS
WINDOW = 128
N_RNN_BLOCKS = 16
RNN_SUPER = 4
LRU_C = 8.0
LN_EPS = 1e-5
HALO = 16
VMEM_LIMIT = 56 * 1024 * 1024


def _params(n_axes):
    return pltpu.CompilerParams(dimension_semantics=("arbitrary",) * n_axes,
                                vmem_limit_bytes=VMEM_LIMIT)


def _resident(shape):
    nd = len(shape)
    return pl.BlockSpec(shape, lambda *_: (0,) * nd, pipeline_mode=pl.Buffered(1))


def _sigmoid(x):
    return jax.nn.sigmoid(x)


def _layer_norm(v, g, b):
    mu = jnp.mean(v, axis=-1, keepdims=True)
    d = v - mu
    var = jnp.mean(d * d, axis=-1, keepdims=True)
    return d * lax.rsqrt(var + LN_EPS) * g + b


def _proj_kernel(x_ref, w_ref, *refs, tn, epilogue, has_bias):
    if has_bias:
        b_ref, o_ref = refs
    else:
        (o_ref,) = refs
    xb = x_ref[...].astype(BF16)
    for c in range(w_ref.shape[1] // tn):
        cs = slice(c * tn, (c + 1) * tn)
        acc = jnp.dot(xb, w_ref[:, cs], preferred_element_type=F32)
        if has_bias:
            acc = acc + b_ref[:, cs]
        o_ref[:, cs] = epilogue(c, acc).astype(o_ref.dtype)


def _proj(x, w, bias, epilogue, *, tm=512, tn=512):
    n, d = x.shape
    nc = w.shape[1]
    in_specs = [pl.BlockSpec((tm, d), lambda i: (i, 0)), _resident((d, nc))]
    args = [x, w]
    if bias is not None:
        in_specs.append(_resident((1, nc)))
        args.append(bias.reshape(1, nc))
    return pl.pallas_call(
        functools.partial(_proj_kernel, tn=tn, epilogue=epilogue, has_bias=bias is not None),
        grid=(n // tm,),
        in_specs=in_specs,
        out_specs=pl.BlockSpec((tm, nc), lambda i: (i, 0)),
        out_shape=jax.ShapeDtypeStruct((n, nc), BF16),
        compiler_params=_params(1),
    )(*args)


def _alibi_slopes():
    h = np.arange(1, N_Q_HEADS + 1, dtype=np.float32)
    return [float(s) for s in (2.0 ** (-8.0 * h / N_Q_HEADS)).astype(np.float32)]


def _split_halves(xp, odd, lo):
    xr = pltpu.roll(xp, HEAD_DIM, axis=1)
    zero = jnp.zeros_like(xp)
    if odd:
        return jnp.where(lo, xr, zero).astype(BF16), jnp.where(lo, zero, xp).astype(BF16)
    return jnp.where(lo, xp, zero).astype(BF16), jnp.where(lo, zero, xr).astype(BF16)


def _attn_kernel(sinks_ref, q_ref, kvp_ref, kvc_ref, o_ref, *, tq, slopes):
    t = pl.program_id(1)
    qi = lax.broadcasted_iota(jnp.int32, (WINDOW, 2 * WINDOW), 0)
    kj = lax.broadcasted_iota(jnp.int32, (WINDOW, 2 * WINDOW), 1)
    dist = WINDOW + qi - kj
    valid = (dist >= 0) & (dist < WINDOW)
    valid_first = valid & ((kj >= WINDOW) | (t > 0))
    distf = dist.astype(F32)
    lo = lax.broadcasted_iota(jnp.int32, (1, 2 * HEAD_DIM), 1) < HEAD_DIM
    kv = jnp.concatenate([kvp_ref[...], kvc_ref[...]], axis=0)
    dk = N_KV_HEADS * HEAD_DIM
    for h in range(N_KV_HEADS):
        pair, odd = divmod(h, 2)
        k_lo, k_hi = _split_halves(kv[:, 128 * pair:128 * pair + 128].astype(F32), odd, lo)
        v_lo, v_hi = _split_halves(kv[:, dk + 128 * pair:dk + 128 * pair + 128].astype(F32), odd, lo)
        bias, bias_first, sink_cols = [], [], []
        for par in range(2):
            blocks, blocks_first, sinks = [], [], []
            for pp in range(GROUP // 2):
                hq = GROUP * h + 2 * pp + par
                ab = slopes[hq] * distf
                blocks.append(jnp.where(valid, ab, jnp.inf))
                blocks_first.append(jnp.where(valid_first, ab, jnp.inf))
                sinks.append(jnp.full((WINDOW, 1), sinks_ref[hq], F32))
            bias.append(jnp.concatenate(blocks, axis=0))
            bias_first.append(jnp.concatenate(blocks_first, axis=0))
            sink_cols.append(jnp.concatenate(sinks, axis=0))
        for qb in range(tq // WINDOW):
            rs = slice(WINDOW * qb, WINDOW * qb + 2 * WINDOW)
            kk = jnp.concatenate([k_lo[rs], k_hi[rs]], axis=0)
            vv = jnp.concatenate([v_lo[rs], v_hi[rs]], axis=0)
            ql = jnp.concatenate(
                [q_ref[WINDOW * qb:WINDOW * (qb + 1), 512 * h + 128 * pp:512 * h + 128 * (pp + 1)]
                 for pp in range(GROUP // 2)], axis=0)
            s = lax.dot_general(ql, kk, (((1,), (1,)), ((), ())), preferred_element_type=F32)
            probs, recips = [], []
            for par in range(2):
                b = bias_first[par] if qb == 0 else bias[par]
                sp = s[:, 2 * WINDOW * par:2 * WINDOW * (par + 1)] - b
                m = jnp.maximum(jnp.max(sp, axis=-1, keepdims=True), sink_cols[par])
                p = jnp.exp(sp - m)
                den = jnp.sum(p, axis=-1, keepdims=True) + jnp.exp(sink_cols[par] - m)
                probs.append(p.astype(BF16))
                recips.append(1.0 / den)
            pv = jnp.dot(jnp.concatenate(probs, axis=1), vv, preferred_element_type=F32)
            out = pv * jnp.where(lo, recips[0], recips[1])
            for pp in range(GROUP // 2):
                o_ref[WINDOW * qb:WINDOW * (qb + 1), 512 * h + 128 * pp:512 * h + 128 * (pp + 1)] = (
                    out[WINDOW * pp:WINDOW * (pp + 1)].astype(o_ref.dtype))


def _attention(qkv, sinks, bsz, seq, *, tq=512):
    n = qkv.shape[0]
    d_attn = N_Q_HEADS * HEAD_DIM
    d_kv2 = 2 * N_KV_HEADS * HEAD_DIM
    nt = seq // tq
    per_w = tq // WINDOW
    kv_col = d_attn // d_kv2
    return pl.pallas_call(
        functools.partial(_attn_kernel, tq=tq, slopes=_alibi_slopes()),
        grid=(bsz, nt),
        in_specs=[
            pl.BlockSpec(memory_space=pltpu.SMEM),
            pl.BlockSpec((tq, d_attn), lambda b, t: (b * nt + t, 0)),
            pl.BlockSpec((WINDOW, d_kv2),
                         lambda b, t: (b * nt * per_w + jnp.maximum(t * per_w - 1, 0), kv_col)),
            pl.BlockSpec((tq, d_kv2), lambda b, t: (b * nt + t, kv_col)),
        ],
        out_specs=pl.BlockSpec((tq, d_attn), lambda b, t: (b * nt + t, 0)),
        out_shape=jax.ShapeDtypeStruct((n, d_attn), BF16),
        compiler_params=_params(2),
    )(sinks, qkv, qkv, qkv)


def _rnn_kernel(rxh_ref, rx_ref, gry_ref, cw_ref, cb_ref, wg_ref, ba_ref, bi_ref, lam_ref, z_ref,
                xs_s, pa_s, pu_s, h_s, carry_s, *, tt):
    t = pl.program_id(1)
    c_all = rx_ref.shape[1]
    csb = c_all // RNN_SUPER

    @pl.when(t == 0)
    def _():
        carry_s[...] = jnp.zeros_like(carry_s)

    xs_s[0:HALO] = jnp.where(t > 0, rxh_ref[...].astype(F32), 0.0)
    xs_s[HALO:HALO + tt] = rx_ref[...].astype(F32)
    width = cw_ref.shape[0]
    xc = cb_ref[...]
    for k in range(width):
        off = HALO - (width - 1) + k
        xc = xc + cw_ref[k:k + 1, :] * xs_s[off:off + tt, :]
    xcb = xc.astype(BF16)

    row = lax.broadcasted_iota(jnp.int32, (tt, csb), 0) % 8
    for sb in range(RNN_SUPER):
        cs = slice(csb * sb, csb * (sb + 1))
        pre = jnp.dot(xcb[:, cs], wg_ref[sb], preferred_element_type=F32)
        r = _sigmoid(pre[:, :csb] + ba_ref[:, cs])
        i = _sigmoid(pre[:, csb:] + bi_ref[:, cs])
        log_a = (-LRU_C) * r * jax.nn.softplus(-lam_ref[:, cs])
        a = jnp.exp(log_a)
        u = jnp.sqrt(-jnp.tanh(log_a) * (a * a + 1.0)) * (i * xc[:, cs])
        for d in (1, 2, 4):
            keep = row >= d
            u_prev = jnp.where(keep, pltpu.roll(u, d, axis=0), 0.0)
            a_prev = jnp.where(keep, pltpu.roll(a, d, axis=0), 1.0)
            u = a * u_prev + u
            a = a * a_prev
        pa_s[:, cs] = a
        pu_s[:, cs] = u

    def body(g, carry):
        r0 = pl.multiple_of(g * 8, 8)
        h = pa_s[pl.ds(r0, 8), :] * carry + pu_s[pl.ds(r0, 8), :]
        h_s[pl.ds(r0, 8), :] = h
        return jnp.broadcast_to(h[7:8, :], h.shape)

    carry_s[...] = lax.fori_loop(0, tt // 8, body, carry_s[...])
    z_ref[...] = (h_s[...] * gry_ref[...].astype(F32)).astype(z_ref.dtype)


def _rnn(rx, gry, conv_w, conv_b, wg, ba, bi, lam, bsz, seq, *, tt=256):
    n, c = rx.shape
    nt = seq // tt
    per_h = tt // HALO
    row = lambda b, t: (b * nt + t, 0)
    vec = lambda a: a.reshape(1, c)
    return pl.pallas_call(
        functools.partial(_rnn_kernel, tt=tt),
        grid=(bsz, nt),
        in_specs=[
            pl.BlockSpec((HALO, c), lambda b, t: (b * nt * per_h + jnp.maximum(t * per_h - 1, 0), 0)),
            pl.BlockSpec((tt, c), row),
            pl.BlockSpec((tt, c), row),
            _resident(conv_w.shape), _resident((1, c)), _resident(wg.shape),
            _resident((1, c)), _resident((1, c)), _resident((1, c)),
        ],
        out_specs=pl.BlockSpec((tt, c), row),
        out_shape=jax.ShapeDtypeStruct((n, c), BF16),
        scratch_shapes=[
            pltpu.VMEM((HALO + tt, c), F32),
            pltpu.VMEM((tt, c), F32), pltpu.VMEM((tt, c), F32), pltpu.VMEM((tt, c), F32),
            pltpu.VMEM((8, c), F32),
        ],
        compiler_params=_params(2),
    )(rx, rx, gry, conv_w, vec(conv_b), wg, vec(ba), vec(bi), vec(lam))


def _gate_superblocks(w):
    nb, bw, _ = w.shape
    per = nb // RNN_SUPER
    w4 = w.reshape(RNN_SUPER, per, bw, bw)
    eye = jnp.eye(per, dtype=w.dtype)
    return jnp.einsum('sicd,ij->sicjd', w4, eye).reshape(RNN_SUPER, per * bw, per * bw)


def _merge_kernel(o_ref, z_ref, g_ref, wa_ref, wr_ref, m_ref, *, tn):
    o = o_ref[...]
    z = z_ref[...]
    d = m_ref.shape[1]
    for c in range(d // tn):
        cs = slice(c * tn, (c + 1) * tn)
        ya = jnp.dot(o, wa_ref[:, cs], preferred_element_type=F32)
        yr = jnp.dot(z, wr_ref[:, cs], preferred_element_type=F32)
        ga = g_ref[:, cs].astype(F32)
        gr = g_ref[:, d + c * tn:d + (c + 1) * tn].astype(F32)
        m_ref[:, cs] = (ga * ya + gr * yr).astype(m_ref.dtype)


def _merge(o, z, gates, wa, wr, *, tm=512, tn=512):
    n, d = o.shape
    return pl.pallas_call(
        functools.partial(_merge_kernel, tn=tn),
        grid=(n // tm,),
        in_specs=[
            pl.BlockSpec((tm, d), lambda i: (i, 0)),
            pl.BlockSpec((tm, z.shape[1]), lambda i: (i, 0)),
            pl.BlockSpec((tm, 2 * d), lambda i: (i, 0)),
            _resident(wa.shape), _resident(wr.shape),
        ],
        out_specs=pl.BlockSpec((tm, d), lambda i: (i, 0)),
        out_shape=jax.ShapeDtypeStruct((n, d), BF16),
        compiler_params=_params(1),
    )(o, z, gates, wa, wr)


def _out_ln_kernel(m_ref, x_ref, w_ref, g_ref, b_ref, y_ref, *, tn, alpha):
    m = m_ref[...]
    for c in range(w_ref.shape[1] // tn):
        cs = slice(c * tn, (c + 1) * tn)
        y_ref[:, cs] = alpha * x_ref[:, cs] + jnp.dot(m, w_ref[:, cs], preferred_element_type=F32)
    y_ref[...] = _layer_norm(y_ref[...], g_ref[...], b_ref[...])


def _out_ln(m, x, w, g, b, alpha, *, tm=512, tn=512):
    n, d = x.shape
    return pl.pallas_call(
        functools.partial(_out_ln_kernel, tn=tn, alpha=alpha),
        grid=(n // tm,),
        in_specs=[
            pl.BlockSpec((tm, m.shape[1]), lambda i: (i, 0)),
            pl.BlockSpec((tm, d), lambda i: (i, 0)),
            _resident(w.shape), _resident((1, d)), _resident((1, d)),
        ],
        out_specs=pl.BlockSpec((tm, d), lambda i: (i, 0)),
        out_shape=jax.ShapeDtypeStruct((n, d), F32),
        compiler_params=_params(1),
    )(m, x, w, g.reshape(1, d), b.reshape(1, d))


def _ffn_kernel(xh_ref, x_ref, wu_ref, wg_ref, cw_ref, cb_ref, wd_ref, g_ref, b_ref, y_ref,
                xb_s, gs_s, *, tm, alpha, blocks_per_seq):
    i = pl.program_id(0)
    c = pl.program_id(1)

    @pl.when(c == 0)
    def _():
        first = (i % blocks_per_seq) == 0
        xb_s[0:HALO] = jnp.where(first, 0.0, xh_ref[...]).astype(BF16)
        xb_s[HALO:HALO + tm] = x_ref[...].astype(BF16)
        y_ref[...] = jnp.zeros_like(y_ref)

    gs_s[...] = jnp.dot(xb_s[...], wg_ref[...], preferred_element_type=F32)
    up = jnp.dot(xb_s[HALO:HALO + tm], wu_ref[...], preferred_element_type=F32)
    width = cw_ref.shape[0]
    gate = cb_ref[...]
    for k in range(width):
        off = HALO - (width - 1) + k
        gate = gate + cw_ref[k:k + 1, :] * gs_s[off:off + tm, :]
    hid = (jax.nn.gelu(gate, approximate=True) * up).astype(BF16)
    y_ref[...] += jnp.dot(hid, wd_ref[...], preferred_element_type=F32)

    @pl.when(c == pl.num_programs(1) - 1)
    def _():
        y_ref[...] = _layer_norm(alpha * x_ref[...] + y_ref[...], g_ref[...], b_ref[...])


def _ffn(x, wu, wg, conv_w, conv_b, wd, g, b, alpha, seq, *, tm=512, tf=512):
    n, d = x.shape
    dff = wu.shape[1]
    per_h = tm // HALO
    return pl.pallas_call(
        functools.partial(_ffn_kernel, tm=tm, alpha=alpha, blocks_per_seq=seq // tm),
        grid=(n // tm, dff // tf),
        in_specs=[
            pl.BlockSpec((HALO, d), lambda i, c: (jnp.maximum(i * per_h - 1, 0), 0)),
            pl.BlockSpec((tm, d), lambda i, c: (i, 0)),
            pl.BlockSpec((d, tf), lambda i, c: (0, c)),
            pl.BlockSpec((d, tf), lambda i, c: (0, c)),
            pl.BlockSpec((conv_w.shape[0], tf), lambda i, c: (0, c)),
            pl.BlockSpec((1, tf), lambda i, c: (0, c)),
            pl.BlockSpec((tf, d), lambda i, c: (c, 0)),
            _resident((1, d)), _resident((1, d)),
        ],
        out_specs=pl.BlockSpec((tm, d), lambda i, c: (i, 0)),
        out_shape=jax.ShapeDtypeStruct((n, d), F32),
        scratch_shapes=[pltpu.VMEM((HALO + tm, d), BF16), pltpu.VMEM((HALO + tm, tf), F32)],
        compiler_params=_params(2),
    )(x, x, wu, wg, conv_w, conv_b.reshape(1, dff), wd, g.reshape(1, d), b.reshape(1, d))


def kernel(x, w_in, b_gate, rnn_conv_w, rnn_conv_b, lru_wa, lru_ba, lru_wi, lru_bi, lru_lambda, attn_sinks,
           w_attn_proj, w_rnn_proj, w_out, ln1_g, ln1_b, ffn_w_up, ffn_w_gate, ffn_conv_w, ffn_conv_b,
           ffn_w_down, ln2_g, ln2_b):
    bsz, seq, d = x.shape
    depth = w_in.shape[0]
    alpha = float((2 * depth) ** 0.25)
    d_attn = N_Q_HEADS * HEAD_DIM
    d_qkv = d_attn + 2 * N_KV_HEADS * HEAD_DIM
    d_rnn = rnn_conv_w.shape[-1]
    q_scale = HEAD_DIM ** -0.5

    def qkv_epilogue(c, acc, tn=512):
        return acc * q_scale if (c + 1) * tn <= d_attn else acc

    h = x.reshape(bsz * seq, d)
    for l in range(depth):
        w = w_in[l].astype(BF16)
        qkv = _proj(h, w[:, :d_qkv], None, qkv_epilogue)
        rx = _proj(h, w[:, d_qkv:d_qkv + d_rnn], None, lambda c, acc: acc)
        gry = _proj(h, w[:, d_qkv + d_rnn:d_qkv + 2 * d_rnn], None,
                    lambda c, acc: jax.nn.gelu(acc, approximate=True))
        gates = _proj(h, w[:, d_qkv + 2 * d_rnn:], b_gate[l], lambda c, acc: _sigmoid(acc))

        o = _attention(qkv, attn_sinks[l], bsz, seq)

        wg = jnp.concatenate([_gate_superblocks(lru_wa[l]), _gate_superblocks(lru_wi[l])], axis=-1).astype(BF16)
        z = _rnn(rx, gry, rnn_conv_w[l], rnn_conv_b[l], wg, lru_ba[l], lru_bi[l], lru_lambda[l], bsz, seq)

        m = _merge(o, z, gates, w_attn_proj[l].astype(BF16), w_rnn_proj[l].astype(BF16))
        h = _out_ln(m, h, w_out[l].astype(BF16), ln1_g[l], ln1_b[l], alpha)
        h = _ffn(h, ffn_w_up[l].astype(BF16), ffn_w_gate[l].astype(BF16), ffn_conv_w[l], ffn_conv_b[l],
                 ffn_w_down[l].astype(BF16), ln2_g[l], ln2_b[l], alpha, seq)
    return h.reshape(bsz, seq, d)
```

```python
import functools

import numpy as np
import jax
import jax.numpy as jnp
from jax import lax
from jax.experimental import pallas as pl
from jax.experimental.pallas import tpu as pltpu

F32 = jnp.float32
BF16 = jnp.bfloat16

HEAD_DIM = 64
N_Q_HEADS = 32
N_KV_HEADS = 4
GROUP = N_Q_HEADS // N_KV_HEADS
WINDOW = 128
N_RNN_BLOCKS = 16
RNN_SUPER = 4
LRU_C = 8.0
LN_EPS = 1e-5
HALO = 16
SUBLANES = 8
VMEM_LIMIT = 56 * 1024 * 1024


def _params(n_axes):
    return pltpu.CompilerParams(dimension_semantics=("arbitrary",) * n_axes,
                                vmem_limit_bytes=VMEM_LIMIT)


def _resident(shape):
    nd = len(shape)
    return pl.BlockSpec(shape, lambda *_: (0,) * nd, pipeline_mode=pl.Buffered(1))


def _sigmoid(x):
    return 0.5 * jnp.tanh(0.5 * x) + 0.5


def _layer_norm(v, g, b):
    mu = jnp.mean(v, axis=-1, keepdims=True)
    d = v - mu
    var = jnp.mean(d * d, axis=-1, keepdims=True)
    return d * lax.rsqrt(var + LN_EPS) * g + b


def _qkv_kernel(x_ref, w_ref, o_ref, xb_ref, *, tn, n_scaled, scale):
    xb = x_ref[...].astype(BF16)
    xb_ref[...] = xb
    for c in range(w_ref.shape[1] // tn):
        cs = slice(c * tn, (c + 1) * tn)
        acc = jnp.dot(xb, w_ref[:, cs], preferred_element_type=F32)
        if c < n_scaled:
            acc = acc * scale
        o_ref[:, cs] = acc.astype(o_ref.dtype)


def _qkv_proj(x, w, d_attn, scale, *, tm=512, tn=512):
    n, d = x.shape
    nc = w.shape[1]
    return pl.pallas_call(
        functools.partial(_qkv_kernel, tn=tn, n_scaled=d_attn // tn, scale=scale),
        grid=(n // tm,),
        in_specs=[pl.BlockSpec((tm, d), lambda i: (i, 0)), _resident((d, nc))],
        out_specs=[pl.BlockSpec((tm, nc), lambda i: (i, 0)), pl.BlockSpec((tm, d), lambda i: (i, 0))],
        out_shape=[jax.ShapeDtypeStruct((n, nc), BF16), jax.ShapeDtypeStruct((n, d), BF16)],
        compiler_params=_params(1),
    )(x, w)


def _gates_kernel(x_ref, w_ref, b_ref, o_ref, *, tn):
    xb = x_ref[...]
    for c in range(w_ref.shape[1] // tn):
        cs = slice(c * tn, (c + 1) * tn)
        acc = jnp.dot(xb, w_ref[:, cs], preferred_element_type=F32) + b_ref[:, cs]
        o_ref[:, cs] = _sigmoid(acc).astype(o_ref.dtype)


def _gates_proj(xb, w, bias, *, tm=512, tn=512):
    n, d = xb.shape
    nc = w.shape[1]
    return pl.pallas_call(
        functools.partial(_gates_kernel, tn=tn),
        grid=(n // tm,),
        in_specs=[pl.BlockSpec((tm, d), lambda i: (i, 0)), _resident((d, nc)), _resident((1, nc))],
        out_specs=pl.BlockSpec((tm, nc), lambda i: (i, 0)),
        out_shape=jax.ShapeDtypeStruct((n, nc), BF16),
        compiler_params=_params(1),
    )(xb, w, bias.reshape(1, nc))


def _alibi_slopes():
    h = np.arange(1, N_Q_HEADS + 1, dtype=np.float32)
    return [float(s) for s in (2.0 ** (-8.0 * h / N_Q_HEADS)).astype(np.float32)]


def _split_halves(xp, odd, lo):
    xr = pltpu.roll(xp, HEAD_DIM, axis=1)
    zero = jnp.zeros_like(xp)
    if odd:
        return jnp.where(lo, xr, zero).astype(BF16), jnp.where(lo, zero, xp).astype(BF16)
    return jnp.where(lo, xp, zero).astype(BF16), jnp.where(lo, zero, xr).astype(BF16)


def _attn_kernel(sinks_ref, q_ref, kvp_ref, kvc_ref, o_ref, p_s, *, tq, slopes):
    t = pl.program_id(1)
    half = GROUP // 2
    qi = lax.broadcasted_iota(jnp.int32, (WINDOW, 2 * WINDOW), 0)
    kj = lax.broadcasted_iota(jnp.int32, (WINDOW, 2 * WINDOW), 1)
    dist = WINDOW + qi - kj
    valid = (dist >= 0) & (dist < WINDOW)
    before_start = (kj < WINDOW) & (t == 0)
    distf = dist.astype(F32)
    lo = lax.broadcasted_iota(jnp.int32, (1, 2 * HEAD_DIM), 1) < HEAD_DIM
    ones_lo = jnp.where(lo, 1.0, 0.0).astype(BF16)
    ones_hi = jnp.where(lo, 0.0, 1.0).astype(BF16)
    kv = jnp.concatenate([kvp_ref[...], kvc_ref[...]], axis=0)
    nkv = kv.shape[0]
    dk = N_KV_HEADS * HEAD_DIM
    for h in range(N_KV_HEADS):
        pair, odd = divmod(h, 2)
        k_lo, k_hi = _split_halves(kv[:, 128 * pair:128 * pair + 128].astype(F32), odd, lo)
        v_lo, v_hi = _split_halves(kv[:, dk + 128 * pair:dk + 128 * pair + 128].astype(F32), odd, lo)
        v_lo = jnp.concatenate([v_lo, jnp.broadcast_to(ones_lo, (nkv, 2 * HEAD_DIM))], axis=1)
        v_hi = jnp.concatenate([v_hi, jnp.broadcast_to(ones_hi, (nkv, 2 * HEAD_DIM))], axis=1)
        bias = [[jnp.where(valid, slopes[GROUP * h + 2 * pp + par] * distf, jnp.inf) for par in range(2)]
                for pp in range(half)]
        for qb in range(tq // WINDOW):
            rs = slice(WINDOW * qb, WINDOW * qb + 2 * WINDOW)
            kk = jnp.concatenate([k_lo[rs], k_hi[rs]], axis=0)
            vv = jnp.concatenate([v_lo[rs], v_hi[rs]], axis=0)
            ql = jnp.concatenate(
                [q_ref[WINDOW * qb:WINDOW * (qb + 1), 512 * h + 128 * pp:512 * h + 128 * (pp + 1)]
                 for pp in range(half)], axis=0)
            s = lax.dot_general(ql, kk, (((1,), (1,)), ((), ())), preferred_element_type=F32)
            sink_terms = []
            for pp in range(half):
                terms = []
                for par in range(2):
                    sink = sinks_ref[GROUP * h + 2 * pp + par]
                    sp = s[WINDOW * pp:WINDOW * (pp + 1), 2 * WINDOW * par:2 * WINDOW * (par + 1)] - bias[pp][par]
                    if qb == 0:
                        sp = jnp.where(before_start, -jnp.inf, sp)
                    m = jnp.maximum(jnp.max(sp, axis=-1, keepdims=True), sink)
                    p_s[WINDOW * pp:WINDOW * (pp + 1), 2 * WINDOW * par:2 * WINDOW * (par + 1)] = (
                        jnp.exp(sp - m).astype(BF16))
                    terms.append(jnp.exp(sink - m))
                sink_terms.append(jnp.where(lo, terms[0], terms[1]))
            pv = jnp.dot(p_s[...], vv, preferred_element_type=F32)
            for pp in range(half):
                rows = slice(WINDOW * pp, WINDOW * (pp + 1))
                den = pv[rows, 2 * HEAD_DIM:] + sink_terms[pp]
                o_ref[WINDOW * qb:WINDOW * (qb + 1), 512 * h + 128 * pp:512 * h + 128 * (pp + 1)] = (
                    pv[rows, :2 * HEAD_DIM] / den).astype(o_ref.dtype)


def _attention(qkv, sinks, bsz, seq, *, tq=512):
    n = qkv.shape[0]
    d_attn = N_Q_HEADS * HEAD_DIM
    d_kv2 = 2 * N_KV_HEADS * HEAD_DIM
    nt = seq // tq
    per_w = tq // WINDOW
    kv_col = d_attn // d_kv2
    return pl.pallas_call(
        functools.partial(_attn_kernel, tq=tq, slopes=_alibi_slopes()),
        grid=(bsz, nt),
        in_specs=[
            pl.BlockSpec(memory_space=pltpu.SMEM),
            pl.BlockSpec((tq, d_attn), lambda b, t: (b * nt + t, 0)),
            pl.BlockSpec((WINDOW, d_kv2),
                         lambda b, t: (b * nt * per_w + jnp.maximum(t * per_w - 1, 0), kv_col)),
            pl.BlockSpec((tq, d_kv2), lambda b, t: (b * nt + t, kv_col)),
        ],
        out_specs=pl.BlockSpec((tq, d_attn), lambda b, t: (b * nt + t, 0)),
        out_shape=jax.ShapeDtypeStruct((n, d_attn), BF16),
        scratch_shapes=[pltpu.VMEM((GROUP // 2 * WINDOW, 4 * WINDOW), BF16)],
        compiler_params=_params(2),
    )(sinks, qkv, qkv, qkv)


def _rnn_kernel(xb_ref, w_ref, cw_ref, cb_ref, wg_ref, ba_ref, bi_ref, lam_ref, z_ref,
                xs_s, gy_s, pa_s, pu_s, h_s, carry_s, *, tt, tn):
    t = pl.program_id(1)
    c_all = z_ref.shape[1]
    csb = c_all // RNN_SUPER
    groups = tt // SUBLANES

    @pl.when(t == 0)
    def _():
        carry_s[...] = jnp.zeros_like(carry_s)
        xs_s[0:HALO] = jnp.zeros((HALO, c_all), F32)

    @pl.when(t > 0)
    def _():
        xs_s[0:HALO] = xs_s[tt:tt + HALO]

    xb = xb_ref[...]
    for c in range(c_all // tn):
        cs = slice(c * tn, (c + 1) * tn)
        xs_s[HALO:HALO + tt, cs] = jnp.dot(xb, w_ref[:, cs], preferred_element_type=F32)
    for c in range(c_all // tn):
        cs = slice(c * tn, (c + 1) * tn)
        ry = jnp.dot(xb, w_ref[:, c_all + c * tn:c_all + (c + 1) * tn], preferred_element_type=F32)
        gy_s[:, cs] = jax.nn.gelu(ry, approximate=True)

    width = cw_ref.shape[0]
    row = lax.broadcasted_iota(jnp.int32, (groups, SUBLANES, csb), 1)
    for sb in range(RNN_SUPER):
        cs = slice(csb * sb, csb * (sb + 1))
        xc = cb_ref[:, cs]
        for k in range(width):
            off = HALO - (width - 1) + k
            xc = xc + cw_ref[k:k + 1, cs] * xs_s[off:off + tt, cs]
        pre = jnp.dot(xc.astype(BF16), wg_ref[sb], preferred_element_type=F32)
        r = _sigmoid(pre[:, :csb] + ba_ref[:, cs])
        i = _sigmoid(pre[:, csb:] + bi_ref[:, cs])
        log_a = (-LRU_C) * r * jax.nn.softplus(-lam_ref[:, cs])
        a = jnp.exp(log_a)
        u = jnp.sqrt(-jnp.tanh(log_a) * (a * a + 1.0)) * (i * xc)
        a = a.reshape(groups, SUBLANES, csb)
        u = u.reshape(groups, SUBLANES, csb)
        for d in (1, 2, 4):
            keep = row >= d
            u_prev = jnp.where(keep, pltpu.roll(u, d, axis=1), 0.0)
            a_prev = jnp.where(keep, pltpu.roll(a, d, axis=1), 1.0)
            u = a * u_prev + u
            a = a * a_prev
        pa_s[:, cs] = a.reshape(tt, csb)
        pu_s[:, cs] = u.reshape(tt, csb)

    def body(g, carry):
        r0 = pl.multiple_of(g * SUBLANES, SUBLANES)
        h = pa_s[pl.ds(r0, SUBLANES), :] * carry + pu_s[pl.ds(r0, SUBLANES), :]
        h_s[pl.ds(r0, SUBLANES), :] = h
        return jnp.broadcast_to(h[SUBLANES - 1:SUBLANES, :], h.shape)

    carry_s[...] = lax.fori_loop(0, groups, body, carry_s[...])
    z_ref[...] = (h_s[...] * gy_s[...]).astype(z_ref.dtype)


def _rnn(xb, w_rxy, conv_w, conv_b, wg, ba, bi, lam, bsz, seq, *, tt=256, tn=512):
    n, d = xb.shape
    c = w_rxy.shape[1] // 2
    nt = seq // tt
    row = lambda b, t: (b * nt + t, 0)
    vec = lambda a: a.reshape(1, c)
    return pl.pallas_call(
        functools.partial(_rnn_kernel, tt=tt, tn=tn),
        grid=(bsz, nt),
        in_specs=[
            pl.BlockSpec((tt, d), row),
            _resident(w_rxy.shape),
            _resident(conv_w.shape), _resident((1, c)), _resident(wg.shape),
            _resident((1, c)), _resident((1, c)), _resident((1, c)),
        ],
        out_specs=pl.BlockSpec((tt, c), row),
        out_shape=jax.ShapeDtypeStruct((n, c), BF16),
        scratch_shapes=[
            pltpu.VMEM((HALO + tt, c), F32),
            pltpu.VMEM((tt, c), F32), pltpu.VMEM((tt, c), F32), pltpu.VMEM((tt, c), F32), pltpu.VMEM((tt, c), F32),
            pltpu.VMEM((SUBLANES, c), F32),
        ],
        compiler_params=_params(2),
    )(xb, w_rxy, conv_w, vec(conv_b), wg, vec(ba), vec(bi), vec(lam))


def _gate_superblocks(w):
    nb, bw, _ = w.shape
    per = nb // RNN_SUPER
    w4 = w.reshape(RNN_SUPER, per, bw, bw)
    eye = jnp.eye(per, dtype=w.dtype)
    return jnp.einsum('sicd,ij->sicjd', w4, eye).reshape(RNN_SUPER, per * bw, per * bw)


def _merge_kernel(o_ref, z_ref, g_ref, wa_ref, wr_ref, m_ref, *, tn):
    o = o_ref[...]
    z = z_ref[...]
    d = m_ref.shape[1]
    for c in range(d // tn):
        cs = slice(c * tn, (c + 1) * tn)
        ya = jnp.dot(o, wa_ref[:, cs], preferred_element_type=F32)
        yr = jnp.dot(z, wr_ref[:, cs], preferred_element_type=F32)
        ga = g_ref[:, cs].astype(F32)
        gr = g_ref[:, d + c * tn:d + (c + 1) * tn].astype(F32)
        m_ref[:, cs] = (ga * ya + gr * yr).astype(m_ref.dtype)


def _merge(o, z, gates, wa, wr, *, tm=512, tn=512):
    n, d = o.shape
    return pl.pallas_call(
        functools.partial(_merge_kernel, tn=tn),
        grid=(n // tm,),
        in_specs=[
            pl.BlockSpec((tm, d), lambda i: (i, 0)),
            pl.BlockSpec((tm, z.shape[1]), lambda i: (i, 0)),
            pl.BlockSpec((tm, 2 * d), lambda i: (i, 0)),
            _resident(wa.shape), _resident(wr.shape),
        ],
        out_specs=pl.BlockSpec((tm, d), lambda i: (i, 0)),
        out_shape=jax.ShapeDtypeStruct((n, d), BF16),
        compiler_params=_params(1),
    )(o, z, gates, wa, wr)


def _out_ln_kernel(m_ref, x_ref, w_ref, g_ref, b_ref, y_ref, *, tn, alpha):
    m = m_ref[...]
    for c in range(w_ref.shape[1] // tn):
        cs = slice(c * tn, (c + 1) * tn)
        y_ref[:, cs] = alpha * x_ref[:, cs] + jnp.dot(m, w_ref[:, cs], preferred_element_type=F32)
    y_ref[...] = _layer_norm(y_ref[...], g_ref[...], b_ref[...])


def _out_ln(m, x, w, g, b, alpha, *, tm=512, tn=512):
    n, d = x.shape
    return pl.pallas_call(
        functools.partial(_out_ln_kernel, tn=tn, alpha=alpha),
        grid=(n // tm,),
        in_specs=[
            pl.BlockSpec((tm, m.shape[1]), lambda i: (i, 0)),
            pl.BlockSpec((tm, d), lambda i: (i, 0)),
            _resident(w.shape), _resident((1, d)), _resident((1, d)),
        ],
        out_specs=pl.BlockSpec((tm, d), lambda i: (i, 0)),
        out_shape=jax.ShapeDtypeStruct((n, d), F32),
        compiler_params=_params(1),
    )(m, x, w, g.reshape(1, d), b.reshape(1, d))


def _ffn_kernel(x_ref, wu_ref, wg_ref, cw_ref, cb_ref, wd_ref, g_ref, b_ref, y_ref,
                xb_s, gs_s, tail_s, *, tm, alpha, blocks_per_seq):
    i = pl.program_id(0)
    c = pl.program_id(1)
    first = (i % blocks_per_seq) == 0

    @pl.when(c == 0)
    def _():
        xb_s[...] = x_ref[...].astype(BF16)
        y_ref[...] = jnp.zeros_like(y_ref)

    @pl.when(first)
    def _():
        gs_s[0:HALO] = jnp.zeros((HALO, gs_s.shape[1]), F32)

    @pl.when(jnp.logical_not(first))
    def _():
        gs_s[0:HALO] = tail_s[c]

    xb = xb_s[...]
    gs_s[HALO:HALO + tm] = jnp.dot(xb, wg_ref[...], preferred_element_type=F32)
    tail_s[c] = gs_s[tm:tm + HALO]
    up = jnp.dot(xb, wu_ref[...], preferred_element_type=F32)
    width = cw_ref.shape[0]
    gate = cb_ref[...]
    for k in range(width):
        off = HALO - (width - 1) + k
        gate = gate + cw_ref[k:k + 1, :] * gs_s[off:off + tm, :]
    hid = (jax.nn.gelu(gate, approximate=True) * up).astype(BF16)
    y_ref[...] += jnp.dot(hid, wd_ref[...], preferred_element_type=F32)

    @pl.when(c == pl.num_programs(1) - 1)
    def _():
        y_ref[...] = _layer_norm(alpha * x_ref[...] + y_ref[...], g_ref[...], b_ref[...])


def _ffn(x, wu, wg, conv_w, conv_b, wd, g, b, alpha, seq, *, tm=512, tf=1024):
    n, d = x.shape
    dff = wu.shape[1]
    return pl.pallas_call(
        functools.partial(_ffn_kernel, tm=tm, alpha=alpha, blocks_per_seq=seq // tm),
        grid=(n // tm, dff // tf),
        in_specs=[
            pl.BlockSpec((tm, d), lambda i, c: (i, 0)),
            pl.BlockSpec((d, tf), lambda i, c: (0, c)),
            pl.BlockSpec((d, tf), lambda i, c: (0, c)),
            pl.BlockSpec((conv_w.shape[0], tf), lambda i, c: (0, c)),
            pl.BlockSpec((1, tf), lambda i, c: (0, c)),
            pl.BlockSpec((tf, d), lambda i, c: (c, 0)),
            _resident((1, d)), _resident((1, d)),
        ],
        out_specs=pl.BlockSpec((tm, d), lambda i, c: (i, 0)),
        out_shape=jax.ShapeDtypeStruct((n, d), F32),
        scratch_shapes=[pltpu.VMEM((tm, d), BF16), pltpu.VMEM((HALO + tm, tf), F32),
                        pltpu.VMEM((dff // tf, HALO, tf), F32)],
        compiler_params=_params(2),
    )(x, wu, wg, conv_w, conv_b.reshape(1, dff), wd, g.reshape(1, d), b.reshape(1, d))


def kernel(x, w_in, b_gate, rnn_conv_w, rnn_conv_b, lru_wa, lru_ba, lru_wi, lru_bi, lru_lambda, attn_sinks,
           w_attn_proj, w_rnn_proj, w_out, ln1_g, ln1_b, ffn_w_up, ffn_w_gate, ffn_conv_w, ffn_conv_b,
           ffn_w_down, ln2_g, ln2_b):
    bsz, seq, d = x.shape
    depth = w_in.shape[0]
    alpha = float((2 * depth) ** 0.25)
    d_attn = N_Q_HEADS * HEAD_DIM
    d_qkv = d_attn + 2 * N_KV_HEADS * HEAD_DIM
    d_rnn = rnn_conv_w.shape[-1]

    h = x.reshape(bsz * seq, d)
    for l in range(depth):
        w = w_in[l].astype(BF16)
        qkv, hb = _qkv_proj(h, w[:, :d_qkv], d_attn, HEAD_DIM ** -0.5)
        gates = _gates_proj(hb, w[:, d_qkv + 2 * d_rnn:], b_gate[l])
        o = _attention(qkv, attn_sinks[l], bsz, seq)

        wg = jnp.concatenate([_gate_superblocks(lru_wa[l]), _gate_superblocks(lru_wi[l])], axis=-1).astype(BF16)
        z = _rnn(hb, w[:, d_qkv:d_qkv + 2 * d_rnn], rnn_conv_w[l], rnn_conv_b[l], wg,
                 lru_ba[l], lru_bi[l], lru_lambda[l], bsz, seq)

        m = _merge(o, z, gates, w_attn_proj[l].astype(BF16), w_rnn_proj[l].astype(BF16))
        h = _out_ln(m, h, w_out[l].astype(BF16), ln1_g[l], ln1_b[l], alpha)
        h = _ffn(h, ffn_w_up[l].astype(BF16), ffn_w_gate[l].astype(BF16), ffn_conv_w[l], ffn_conv_b[l],
                 ffn_w_down[l].astype(BF16), ln2_g[l], ln2_b[l], alpha, seq)
    return h.reshape(bsz, seq, d)
```

```python
import functools

import numpy as np
import jax
import jax.numpy as jnp
from jax import lax
from jax.experimental import pallas as pl
from jax.experimental.pallas import tpu as pltpu

F32 = jnp.float32
BF16 = jnp.bfloat16

HEAD_DIM = 64
N_Q_HEADS = 32
N_KV_HEADS = 4
GROUP = N_Q_HEADS // N_KV_HEADS
WINDOW = 128
N_RNN_BLOCKS = 16
RNN_SUPER = 4
LRU_C = 8.0
LN_EPS = 1e-5
HALO = 16
SUBLANES = 8
LANES = 128
VMEM_LIMIT = 56 * 1024 * 1024


def _params(n_axes, vmem_limit=VMEM_LIMIT):
    return pltpu.CompilerParams(dimension_semantics=("arbitrary",) * n_axes,
                                vmem_limit_bytes=vmem_limit)


def _resident(shape):
    nd = len(shape)
    return pl.BlockSpec(shape, lambda *_: (0,) * nd, pipeline_mode=pl.Buffered(1))


def _sigmoid(x):
    return 0.5 * jnp.tanh(0.5 * x) + 0.5


def _layer_norm(v, g, b):
    mu = jnp.mean(v, axis=-1, keepdims=True)
    d = v - mu
    var = jnp.mean(d * d, axis=-1, keepdims=True)
    return d * lax.rsqrt(var + LN_EPS) * g + b


def _qkv_kernel(x_ref, w_ref, o_ref, xb_ref, *, tn, n_scaled, scale):
    xb = x_ref[...].astype(BF16)
    xb_ref[...] = xb
    for c in range(w_ref.shape[1] // tn):
        cs = slice(c * tn, (c + 1) * tn)
        acc = jnp.dot(xb, w_ref[:, cs], preferred_element_type=F32)
        if c < n_scaled:
            acc = acc * scale
        o_ref[:, cs] = acc.astype(o_ref.dtype)


def _qkv_proj(x, w, d_attn, scale, *, tm=512, tn=512):
    n, d = x.shape
    nc = w.shape[1]
    return pl.pallas_call(
        functools.partial(_qkv_kernel, tn=tn, n_scaled=d_attn // tn, scale=scale),
        grid=(n // tm,),
        in_specs=[pl.BlockSpec((tm, d), lambda i: (i, 0)), _resident((d, nc))],
        out_specs=[pl.BlockSpec((tm, nc), lambda i: (i, 0)), pl.BlockSpec((tm, d), lambda i: (i, 0))],
        out_shape=[jax.ShapeDtypeStruct((n, nc), BF16), jax.ShapeDtypeStruct((n, d), BF16)],
        compiler_params=_params(1),
    )(x, w)


def _gates_kernel(x_ref, w_ref, b_ref, o_ref, *, tn):
    xb = x_ref[...]
    for c in range(w_ref.shape[1] // tn):
        cs = slice(c * tn, (c + 1) * tn)
        acc = jnp.dot(xb, w_ref[:, cs], preferred_element_type=F32) + b_ref[:, cs]
        o_ref[:, cs] = _sigmoid(acc).astype(o_ref.dtype)


def _gates_proj(xb, w, bias, *, tm=512, tn=512):
    n, d = xb.shape
    nc = w.shape[1]
    return pl.pallas_call(
        functools.partial(_gates_kernel, tn=tn),
        grid=(n // tm,),
        in_specs=[pl.BlockSpec((tm, d), lambda i: (i, 0)), _resident((d, nc)), _resident((1, nc))],
        out_specs=pl.BlockSpec((tm, nc), lambda i: (i, 0)),
        out_shape=jax.ShapeDtypeStruct((n, nc), BF16),
        compiler_params=_params(1),
    )(xb, w, bias.reshape(1, nc))


def _alibi_slopes():
    h = np.arange(1, N_Q_HEADS + 1, dtype=np.float32)
    return [float(s) for s in (2.0 ** (-8.0 * h / N_Q_HEADS)).astype(np.float32)]


def _split_halves(xp, odd, lo):
    xr = pltpu.roll(xp, HEAD_DIM, axis=1)
    zero = jnp.zeros_like(xp)
    if odd:
        return jnp.where(lo, xr, zero).astype(BF16), jnp.where(lo, zero, xp).astype(BF16)
    return jnp.where(lo, xp, zero).astype(BF16), jnp.where(lo, zero, xr).astype(BF16)


def _attn_kernel(sinks_ref, q_ref, kvp_ref, kvc_ref, o_ref, p_s, *, tq, slopes):
    t = pl.program_id(1)
    half = GROUP // 2
    qi = lax.broadcasted_iota(jnp.int32, (WINDOW, 2 * WINDOW), 0)
    kj = lax.broadcasted_iota(jnp.int32, (WINDOW, 2 * WINDOW), 1)
    dist = WINDOW + qi - kj
    valid = (dist >= 0) & (dist < WINDOW)
    before_start = (kj < WINDOW) & (t == 0)
    distf = dist.astype(F32)
    lo = lax.broadcasted_iota(jnp.int32, (1, 2 * HEAD_DIM), 1) < HEAD_DIM
    ones_lo = jnp.where(lo, 1.0, 0.0).astype(BF16)
    ones_hi = jnp.where(lo, 0.0, 1.0).astype(BF16)
    kv = jnp.concatenate([kvp_ref[...], kvc_ref[...]], axis=0)
    nkv = kv.shape[0]
    dk = N_KV_HEADS * HEAD_DIM
    for h in range(N_KV_HEADS):
        pair, odd = divmod(h, 2)
        k_lo, k_hi = _split_halves(kv[:, 128 * pair:128 * pair + 128].astype(F32), odd, lo)
        v_lo, v_hi = _split_halves(kv[:, dk + 128 * pair:dk + 128 * pair + 128].astype(F32), odd, lo)
        v_lo = jnp.concatenate([v_lo, jnp.broadcast_to(ones_lo, (nkv, 2 * HEAD_DIM))], axis=1)
        v_hi = jnp.concatenate([v_hi, jnp.broadcast_to(ones_hi, (nkv, 2 * HEAD_DIM))], axis=1)
        bias = [[jnp.where(valid, slopes[GROUP * h + 2 * pp + par] * distf, jnp.inf) for par in range(2)]
                for pp in range(half)]
        for qb in range(tq // WINDOW):
            rs = slice(WINDOW * qb, WINDOW * qb + 2 * WINDOW)
            kk = jnp.concatenate([k_lo[rs], k_hi[rs]], axis=0)
            vv = jnp.concatenate([v_lo[rs], v_hi[rs]], axis=0)
            ql = jnp.concatenate(
                [q_ref[WINDOW * qb:WINDOW * (qb + 1), 512 * h + 128 * pp:512 * h + 128 * (pp + 1)]
                 for pp in range(half)], axis=0)
            s = lax.dot_general(ql, kk, (((1,), (1,)), ((), ())), preferred_element_type=F32)
            sink_terms = []
            for pp in range(half):
                terms = []
                for par in range(2):
                    sink = sinks_ref[GROUP * h + 2 * pp + par]
                    sp = s[WINDOW * pp:WINDOW * (pp + 1), 2 * WINDOW * par:2 * WINDOW * (par + 1)] - bias[pp][par]
                    if qb == 0:
                        sp = jnp.where(before_start, -jnp.inf, sp)
                    m = jnp.maximum(jnp.max(sp, axis=-1, keepdims=True), sink)
                    p_s[WINDOW * pp:WINDOW * (pp + 1), 2 * WINDOW * par:2 * WINDOW * (par + 1)] = (
                        jnp.exp(sp - m).astype(BF16))
                    terms.append(jnp.exp(sink - m))
                sink_terms.append(jnp.where(lo, terms[0], terms[1]))
            pv = jnp.dot(p_s[...], vv, preferred_element_type=F32)
            for pp in range(half):
                rows = slice(WINDOW * pp, WINDOW * (pp + 1))
                den = pv[rows, 2 * HEAD_DIM:] + sink_terms[pp]
                o_ref[WINDOW * qb:WINDOW * (qb + 1), 512 * h + 128 * pp:512 * h + 128 * (pp + 1)] = (
                    pv[rows, :2 * HEAD_DIM] / den).astype(o_ref.dtype)


def _attention(qkv, sinks, bsz, seq, *, tq=512):
    n = qkv.shape[0]
    d_attn = N_Q_HEADS * HEAD_DIM
    d_kv2 = 2 * N_KV_HEADS * HEAD_DIM
    nt = seq // tq
    per_w = tq // WINDOW
    kv_col = d_attn // d_kv2
    return pl.pallas_call(
        functools.partial(_attn_kernel, tq=tq, slopes=_alibi_slopes()),
        grid=(bsz, nt),
        in_specs=[
            pl.BlockSpec(memory_space=pltpu.SMEM),
            pl.BlockSpec((tq, d_attn), lambda b, t: (b * nt + t, 0)),
            pl.BlockSpec((WINDOW, d_kv2),
                         lambda b, t: (b * nt * per_w + jnp.maximum(t * per_w - 1, 0), kv_col)),
            pl.BlockSpec((tq, d_kv2), lambda b, t: (b * nt + t, kv_col)),
        ],
        out_specs=pl.BlockSpec((tq, d_attn), lambda b, t: (b * nt + t, 0)),
        out_shape=jax.ShapeDtypeStruct((n, d_attn), BF16),
        scratch_shapes=[pltpu.VMEM((GROUP // 2 * WINDOW, 4 * WINDOW), BF16)],
        compiler_params=_params(2),
    )(sinks, qkv, qkv, qkv)


def _rnn_kernel(x_ref, w_ref, cw_ref, cb_ref, wg_ref, ba_ref, bi_ref, lam_ref, z_ref,
                rx_s, gy_s, xc_s, pa_s, pu_s, hp_s, tail_s, carry_s, *, tt, tn):
    t = pl.program_id(1)
    c_all = z_ref.shape[1]
    csb = c_all // RNN_SUPER
    n_slabs = c_all // LANES
    per_chunk = tn // LANES
    ng = tt // SUBLANES
    width = cw_ref.shape[0]

    @pl.when(t == 0)
    def _():
        carry_s[...] = jnp.zeros_like(carry_s)
        tail_s[...] = jnp.zeros_like(tail_s)

    xb = x_ref[...].astype(BF16)
    for c in range(c_all // tn):
        acc = jnp.dot(xb, w_ref[:, c * tn:(c + 1) * tn], preferred_element_type=F32)
        for k in range(per_chunk):
            rx_s[c * per_chunk + k] = acc[:, LANES * k:LANES * (k + 1)]
    for c in range(c_all // tn):
        ry = jnp.dot(xb, w_ref[:, c_all + c * tn:c_all + (c + 1) * tn], preferred_element_type=F32)
        gy = jax.nn.gelu(ry, approximate=True)
        for k in range(per_chunk):
            gy_s[c * per_chunk + k] = gy[:, LANES * k:LANES * (k + 1)]

    first_row = lax.broadcasted_iota(jnp.int32, (ng, LANES), 0) == 0
    for s in range(n_slabs):
        ls = slice(LANES * s, LANES * (s + 1))
        x_ph = [rx_s[s, pl.ds(j, ng, stride=SUBLANES), :] for j in range(SUBLANES)]
        x_prev = {m: jnp.where(first_row, tail_s[s, m:m + 1, :], pltpu.roll(x_ph[m], 1, axis=0))
                  for m in range(SUBLANES - (width - 1), SUBLANES)}
        for j in range(SUBLANES):
            acc = cb_ref[:, ls]
            for k in range(width):
                src = j - (width - 1) + k
                acc = acc + cw_ref[k:k + 1, ls] * (x_ph[src] if src >= 0 else x_prev[src + SUBLANES])
            xc_s[ng * j:ng * (j + 1), ls] = acc
        tail_s[s] = rx_s[s, tt - SUBLANES:tt, :]

    for sb in range(RNN_SUPER):
        cs = slice(csb * sb, csb * (sb + 1))
        xc = xc_s[:, cs]
        pre = jnp.dot(xc.astype(BF16), wg_ref[sb], preferred_element_type=F32)
        r = _sigmoid(pre[:, :csb] + ba_ref[:, cs])
        i = _sigmoid(pre[:, csb:] + bi_ref[:, cs])
        log_a = r * ((-LRU_C) * jax.nn.softplus(-lam_ref[:, cs]))
        a = jnp.exp(log_a)
        u = jnp.sqrt(-jnp.tanh(log_a) * (a * a + 1.0)) * (i * xc)
        pa = a[0:ng]
        pu = u[0:ng]
        pa_s[0:ng, cs] = pa
        pu_s[0:ng, cs] = pu
        for j in range(1, SUBLANES):
            rows = slice(ng * j, ng * (j + 1))
            pu = a[rows] * pu + u[rows]
            pa = a[rows] * pa
            pa_s[rows, cs] = pa
            pu_s[rows, cs] = pu

    last = ng * (SUBLANES - 1)

    def body(g, carry):
        hp_s[pl.ds(g, 1), :] = carry
        return pa_s[pl.ds(last + g, 1), :] * carry + pu_s[pl.ds(last + g, 1), :]

    carry_s[...] = lax.fori_loop(0, ng, body, carry_s[...])

    for s in range(n_slabs):
        ls = slice(LANES * s, LANES * (s + 1))
        hp = hp_s[:, ls]
        for j in range(SUBLANES):
            rows = slice(ng * j, ng * (j + 1))
            h = pa_s[rows, ls] * hp + pu_s[rows, ls]
            rx_s[s, pl.ds(j, ng, stride=SUBLANES), :] = h * gy_s[s, pl.ds(j, ng, stride=SUBLANES), :]
        z_ref[:, ls] = rx_s[s].astype(z_ref.dtype)


def _rnn(x, w_rxy, conv_w, conv_b, wg, ba, bi, lam, bsz, seq, *, tt=256, tn=512):
    n, d = x.shape
    c = w_rxy.shape[1] // 2
    nt = seq // tt
    row = lambda b, t: (b * nt + t, 0)
    vec = lambda a: a.reshape(1, c)
    slab = lambda rows: pltpu.VMEM((c // LANES, rows, LANES), F32)
    return pl.pallas_call(
        functools.partial(_rnn_kernel, tt=tt, tn=tn),
        grid=(bsz, nt),
        in_specs=[
            pl.BlockSpec((tt, d), row),
            _resident(w_rxy.shape),
            _resident(conv_w.shape), _resident((1, c)), _resident(wg.shape),
            _resident((1, c)), _resident((1, c)), _resident((1, c)),
        ],
        out_specs=pl.BlockSpec((tt, c), row),
        out_shape=jax.ShapeDtypeStruct((n, c), BF16),
        scratch_shapes=[
            slab(tt), slab(tt),
            pltpu.VMEM((tt, c), F32), pltpu.VMEM((tt, c), F32), pltpu.VMEM((tt, c), F32),
            pltpu.VMEM((tt // SUBLANES, c), F32),
            slab(SUBLANES),
            pltpu.VMEM((1, c), F32),
        ],
        compiler_params=_params(2),
    )(x, w_rxy, conv_w, vec(conv_b), wg, vec(ba), vec(bi), vec(lam))


def _gate_superblocks(w):
    nb, bw, _ = w.shape
    per = nb // RNN_SUPER
    w4 = w.reshape(RNN_SUPER, per, bw, bw)
    eye = jnp.eye(per, dtype=w.dtype)
    return jnp.einsum('sicd,ij->sicjd', w4, eye).reshape(RNN_SUPER, per * bw, per * bw)


def _merge_kernel(o_ref, z_ref, g_ref, wa_ref, wr_ref, m_ref, *, tn):
    o = o_ref[...]
    z = z_ref[...]
    d = m_ref.shape[1]
    for c in range(d // tn):
        cs = slice(c * tn, (c + 1) * tn)
        ya = jnp.dot(o, wa_ref[:, cs], preferred_element_type=F32)
        yr = jnp.dot(z, wr_ref[:, cs], preferred_element_type=F32)
        ga = g_ref[:, cs].astype(F32)
        gr = g_ref[:, d + c * tn:d + (c + 1) * tn].astype(F32)
        m_ref[:, cs] = (ga * ya + gr * yr).astype(m_ref.dtype)


def _merge(o, z, gates, wa, wr, *, tm=512, tn=512):
    n, d = o.shape
    return pl.pallas_call(
        functools.partial(_merge_kernel, tn=tn),
        grid=(n // tm,),
        in_specs=[
            pl.BlockSpec((tm, d), lambda i: (i, 0)),
            pl.BlockSpec((tm, z.shape[1]), lambda i: (i, 0)),
            pl.BlockSpec((tm, 2 * d), lambda i: (i, 0)),
            _resident(wa.shape), _resident(wr.shape),
        ],
        out_specs=pl.BlockSpec((tm, d), lambda i: (i, 0)),
        out_shape=jax.ShapeDtypeStruct((n, d), BF16),
        compiler_params=_params(1),
    )(o, z, gates, wa, wr)


def _out_ln_kernel(m_ref, x_ref, w_ref, g_ref, b_ref, y_ref, *, tn, alpha):
    m = m_ref[...]
    for c in range(w_ref.shape[1] // tn):
        cs = slice(c * tn, (c + 1) * tn)
        y_ref[:, cs] = alpha * x_ref[:, cs] + jnp.dot(m, w_ref[:, cs], preferred_element_type=F32)
    y_ref[...] = _layer_norm(y_ref[...], g_ref[...], b_ref[...])


def _out_ln(m, x, w, g, b, alpha, *, tm=512, tn=512):
    n, d = x.shape
    return pl.pallas_call(
        functools.partial(_out_ln_kernel, tn=tn, alpha=alpha),
        grid=(n // tm,),
        in_specs=[
            pl.BlockSpec((tm, m.shape[1]), lambda i: (i, 0)),
            pl.BlockSpec((tm, d), lambda i: (i, 0)),
            _resident(w.shape), _resident((1, d)), _resident((1, d)),
        ],
        out_specs=pl.BlockSpec((tm, d), lambda i: (i, 0)),
        out_shape=jax.ShapeDtypeStruct((n, d), F32),
        compiler_params=_params(1),
    )(m, x, w, g.reshape(1, d), b.reshape(1, d))


def _ffn_kernel(x_ref, wu_ref, wg_ref, cw_ref, cb_ref, wd_ref, g_ref, b_ref, y_ref,
                xb_s, gs_s, tail_s, *, tm, alpha, blocks_per_seq):
    i = pl.program_id(0)
    c = pl.program_id(1)
    first = (i % blocks_per_seq) == 0

    @pl.when(c == 0)
    def _():
        xb_s[...] = x_ref[...].astype(BF16)
        y_ref[...] = jnp.zeros_like(y_ref)

    @pl.when(first)
    def _():
        gs_s[0:HALO] = jnp.zeros((HALO, gs_s.shape[1]), F32)

    @pl.when(jnp.logical_not(first))
    def _():
        gs_s[0:HALO] = tail_s[c]

    xb = xb_s[...]
    gs_s[HALO:HALO + tm] = jnp.dot(xb, wg_ref[...], preferred_element_type=F32)
    tail_s[c] = gs_s[tm:tm + HALO]
    up = jnp.dot(xb, wu_ref[...], preferred_element_type=F32)
    width = cw_ref.shape[0]
    gate = cb_ref[...]
    for k in range(width):
        off = HALO - (width - 1) + k
        gate = gate + cw_ref[k:k + 1, :] * gs_s[off:off + tm, :]
    hid = (jax.nn.gelu(gate, approximate=True) * up).astype(BF16)
    y_ref[...] += jnp.dot(hid, wd_ref[...], preferred_element_type=F32)

    @pl.when(c == pl.num_programs(1) - 1)
    def _():
        y_ref[...] = _layer_norm(alpha * x_ref[...] + y_ref[...], g_ref[...], b_ref[...])


def _ffn(x, wu, wg, conv_w, conv_b, wd, g, b, alpha, seq, *, tm=512, tf=1024):
    n, d = x.shape
    dff = wu.shape[1]
    return pl.pallas_call(
        functools.partial(_ffn_kernel, tm=tm, alpha=alpha, blocks_per_seq=seq // tm),
        grid=(n // tm, dff // tf),
        in_specs=[
            pl.BlockSpec((tm, d), lambda i, c: (i, 0)),
            pl.BlockSpec((d, tf), lambda i, c: (0, c)),
            pl.BlockSpec((d, tf), lambda i, c: (0, c)),
            pl.BlockSpec((conv_w.shape[0], tf), lambda i, c: (0, c)),
            pl.BlockSpec((1, tf), lambda i, c: (0, c)),
            pl.BlockSpec((tf, d), lambda i, c: (c, 0)),
            _resident((1, d)), _resident((1, d)),
        ],
        out_specs=pl.BlockSpec((tm, d), lambda i, c: (i, 0)),
        out_shape=jax.ShapeDtypeStruct((n, d), F32),
        scratch_shapes=[pltpu.VMEM((tm, d), BF16), pltpu.VMEM((HALO + tm, tf), F32),
                        pltpu.VMEM((dff // tf, HALO, tf), F32)],
        compiler_params=_params(2),
    )(x, wu, wg, conv_w, conv_b.reshape(1, dff), wd, g.reshape(1, d), b.reshape(1, d))


def kernel(x, w_in, b_gate, rnn_conv_w, rnn_conv_b, lru_wa, lru_ba, lru_wi, lru_bi, lru_lambda, attn_sinks,
           w_attn_proj, w_rnn_proj, w_out, ln1_g, ln1_b, ffn_w_up, ffn_w_gate, ffn_conv_w, ffn_conv_b,
           ffn_w_down, ln2_g, ln2_b):
    bsz, seq, d = x.shape
    depth = w_in.shape[0]
    alpha = float((2 * depth) ** 0.25)
    d_attn = N_Q_HEADS * HEAD_DIM
    d_qkv = d_attn + 2 * N_KV_HEADS * HEAD_DIM
    d_rnn = rnn_conv_w.shape[-1]

    h = x.reshape(bsz * seq, d)
    for l in range(depth):
        w_qkv = w_in[l, :, :d_qkv].astype(BF16)
        w_rxy = w_in[l, :, d_qkv:d_qkv + 2 * d_rnn].astype(BF16)
        w_gl = w_in[l, :, d_qkv + 2 * d_rnn:].astype(BF16)
        qkv, hb = _qkv_proj(h, w_qkv, d_attn, HEAD_DIM ** -0.5)
        gates = _gates_proj(hb, w_gl, b_gate[l])
        o = _attention(qkv, attn_sinks[l], bsz, seq)

        wg = jnp.concatenate([_gate_superblocks(lru_wa[l]), _gate_superblocks(lru_wi[l])], axis=-1).astype(BF16)
        z = _rnn(h, w_rxy, rnn_conv_w[l], rnn_conv_b[l], wg, lru_ba[l], lru_bi[l], lru_lambda[l], bsz, seq)

        m = _merge(o, z, gates, w_attn_proj[l].astype(BF16), w_rnn_proj[l].astype(BF16))
        h = _out_ln(m, h, w_out[l].astype(BF16), ln1_g[l], ln1_b[l], alpha)
        h = _ffn(h, ffn_w_up[l].astype(BF16), ffn_w_gate[l].astype(BF16), ffn_conv_w[l], ffn_conv_b[l],
                 ffn_w_down[l].astype(BF16), ln2_g[l], ln2_b[l], alpha, seq)
    return h.reshape(bsz, seq, d)
```

```python
import functools

import numpy as np
import jax
import jax.numpy as jnp
from jax import lax
from jax.experimental import pallas as pl
from jax.experimental.pallas import tpu as pltpu

F32 = jnp.float32
BF16 = jnp.bfloat16

HEAD_DIM = 64
N_Q_HEADS = 32
N_KV_HEADS = 4
GROUP = N_Q_HEADS // N_KV_HEADS
WINDOW = 128
N_RNN_BLOCKS = 16
RNN_SUPER = 4
LRU_C = 8.0
LN_EPS = 1e-5
HALO = 16
SUBLANES = 8
LANES = 128
ROW_SUB = 512
VMEM_LIMIT = 56 * 1024 * 1024


def _params(n_axes, vmem_limit=VMEM_LIMIT):
    return pltpu.CompilerParams(dimension_semantics=("arbitrary",) * n_axes,
                                vmem_limit_bytes=vmem_limit)


def _resident(shape):
    nd = len(shape)
    return pl.BlockSpec(shape, lambda *_: (0,) * nd, pipeline_mode=pl.Buffered(1))


def _sigmoid(x):
    return 0.5 * jnp.tanh(0.5 * x) + 0.5


def _layer_norm(v, g, b):
    mu = jnp.mean(v, axis=-1, keepdims=True)
    d = v - mu
    var = jnp.mean(d * d, axis=-1, keepdims=True)
    return d * lax.rsqrt(var + LN_EPS) * g + b


def _cast_kernel(w_ref, *o_refs, splits):
    for o_ref, (lo, hi) in zip(o_refs, splits):
        o_ref[...] = w_ref[:, lo:hi].astype(o_ref.dtype)


def _cast_bf16(w, bounds=None, *, block_bytes=8 * 1024 * 1024):
    rows, cols = w.shape
    bounds = (0, cols) if bounds is None else tuple(bounds)
    splits = tuple(zip(bounds[:-1], bounds[1:]))
    tr = rows
    while tr * cols * 4 > block_bytes and tr % 32 == 0:
        tr //= 2
    out = pl.pallas_call(
        functools.partial(_cast_kernel, splits=splits),
        grid=(rows // tr,),
        in_specs=[pl.BlockSpec((tr, cols), lambda i: (i, 0))],
        out_specs=[pl.BlockSpec((tr, hi - lo), lambda i: (i, 0)) for lo, hi in splits],
        out_shape=[jax.ShapeDtypeStruct((rows, hi - lo), BF16) for lo, hi in splits],
        compiler_params=_params(1),
    )(w)
    return out[0] if len(out) == 1 else out


def _qkv_kernel(x_ref, w_ref, o_ref, xb_ref, *, tn, n_scaled, scale):
    for r in range(x_ref.shape[0] // ROW_SUB):
        rs = slice(r * ROW_SUB, (r + 1) * ROW_SUB)
        xb = x_ref[rs, :].astype(BF16)
        xb_ref[rs, :] = xb
        for c in range(w_ref.shape[1] // tn):
            cs = slice(c * tn, (c + 1) * tn)
            acc = jnp.dot(xb, w_ref[:, cs], preferred_element_type=F32)
            if c < n_scaled:
                acc = acc * scale
            o_ref[rs, cs] = acc.astype(o_ref.dtype)


def _qkv_proj(x, w, d_attn, scale, *, tm=1024, tn=512):
    n, d = x.shape
    nc = w.shape[1]
    return pl.pallas_call(
        functools.partial(_qkv_kernel, tn=tn, n_scaled=d_attn // tn, scale=scale),
        grid=(n // tm,),
        in_specs=[pl.BlockSpec((tm, d), lambda i: (i, 0)), _resident((d, nc))],
        out_specs=[pl.BlockSpec((tm, nc), lambda i: (i, 0)), pl.BlockSpec((tm, d), lambda i: (i, 0))],
        out_shape=[jax.ShapeDtypeStruct((n, nc), BF16), jax.ShapeDtypeStruct((n, d), BF16)],
        compiler_params=_params(1),
    )(x, w)


def _gates_kernel(x_ref, w_ref, b_ref, o_ref, *, tn):
    for r in range(x_ref.shape[0] // ROW_SUB):
        rs = slice(r * ROW_SUB, (r + 1) * ROW_SUB)
        xb = x_ref[rs, :]
        for c in range(w_ref.shape[1] // tn):
            cs = slice(c * tn, (c + 1) * tn)
            acc = jnp.dot(xb, w_ref[:, cs], preferred_element_type=F32) + b_ref[:, cs]
            o_ref[rs, cs] = _sigmoid(acc).astype(o_ref.dtype)


def _gates_proj(xb, w, bias, *, tm=1024, tn=512):
    n, d = xb.shape
    nc = w.shape[1]
    return pl.pallas_call(
        functools.partial(_gates_kernel, tn=tn),
        grid=(n // tm,),
        in_specs=[pl.BlockSpec((tm, d), lambda i: (i, 0)), _resident((d, nc)), _resident((1, nc))],
        out_specs=pl.BlockSpec((tm, nc), lambda i: (i, 0)),
        out_shape=jax.ShapeDtypeStruct((n, nc), BF16),
        compiler_params=_params(1),
    )(xb, w, bias.reshape(1, nc))


def _alibi_slopes():
    h = np.arange(1, N_Q_HEADS + 1, dtype=np.float32)
    return [float(s) for s in (2.0 ** (-8.0 * h / N_Q_HEADS)).astype(np.float32)]


def _split_halves(xp, odd, lo):
    xr = pltpu.roll(xp, HEAD_DIM, axis=1)
    zero = jnp.zeros_like(xp)
    if odd:
        return jnp.where(lo, xr, zero).astype(BF16), jnp.where(lo, zero, xp).astype(BF16)
    return jnp.where(lo, xp, zero).astype(BF16), jnp.where(lo, zero, xr).astype(BF16)


def _attn_kernel(sinks_ref, q_ref, kvp_ref, kvc_ref, o_ref, p_s, *, tq, slopes):
    t = pl.program_id(1)
    half = GROUP // 2
    qi = lax.broadcasted_iota(jnp.int32, (WINDOW, 2 * WINDOW), 0)
    kj = lax.broadcasted_iota(jnp.int32, (WINDOW, 2 * WINDOW), 1)
    dist = WINDOW + qi - kj
    valid = (dist >= 0) & (dist < WINDOW)
    before_start = (kj < WINDOW) & (t == 0)
    distf = dist.astype(F32)
    lo = lax.broadcasted_iota(jnp.int32, (1, 2 * HEAD_DIM), 1) < HEAD_DIM
    ones_lo = jnp.where(lo, 1.0, 0.0).astype(BF16)
    ones_hi = jnp.where(lo, 0.0, 1.0).astype(BF16)
    kv = jnp.concatenate([kvp_ref[...], kvc_ref[...]], axis=0)
    nkv = kv.shape[0]
    dk = N_KV_HEADS * HEAD_DIM
    for h in range(N_KV_HEADS):
        pair, odd = divmod(h, 2)
        k_lo, k_hi = _split_halves(kv[:, 128 * pair:128 * pair + 128].astype(F32), odd, lo)
        v_lo, v_hi = _split_halves(kv[:, dk + 128 * pair:dk + 128 * pair + 128].astype(F32), odd, lo)
        v_lo = jnp.concatenate([v_lo, jnp.broadcast_to(ones_lo, (nkv, 2 * HEAD_DIM))], axis=1)
        v_hi = jnp.concatenate([v_hi, jnp.broadcast_to(ones_hi, (nkv, 2 * HEAD_DIM))], axis=1)
        bias = [[jnp.where(valid, slopes[GROUP * h + 2 * pp + par] * distf, jnp.inf) for par in range(2)]
                for pp in range(half)]
        for qb in range(tq // WINDOW):
            rs = slice(WINDOW * qb, WINDOW * qb + 2 * WINDOW)
            kk = jnp.concatenate([k_lo[rs], k_hi[rs]], axis=0)
            vv = jnp.concatenate([v_lo[rs], v_hi[rs]], axis=0)
            ql = jnp.concatenate(
                [q_ref[WINDOW * qb:WINDOW * (qb + 1), 512 * h + 128 * pp:512 * h + 128 * (pp + 1)]
                 for pp in range(half)], axis=0)
            s = lax.dot_general(ql, kk, (((1,), (1,)), ((), ())), preferred_element_type=F32)
            sink_terms = []
            for pp in range(half):
                terms = []
                for par in range(2):
                    sink = sinks_ref[GROUP * h + 2 * pp + par]
                    sp = s[WINDOW * pp:WINDOW * (pp + 1), 2 * WINDOW * par:2 * WINDOW * (par + 1)] - bias[pp][par]
                    if qb == 0:
                        sp = jnp.where(before_start, -jnp.inf, sp)
                    m = jnp.maximum(jnp.max(sp, axis=-1, keepdims=True), sink)
                    p_s[WINDOW * pp:WINDOW * (pp + 1), 2 * WINDOW * par:2 * WINDOW * (par + 1)] = (
                        jnp.exp(sp - m).astype(BF16))
                    terms.append(jnp.exp(sink - m))
                sink_terms.append(jnp.where(lo, terms[0], terms[1]))
            pv = jnp.dot(p_s[...], vv, preferred_element_type=F32)
            for pp in range(half):
                rows = slice(WINDOW * pp, WINDOW * (pp + 1))
                den = pv[rows, 2 * HEAD_DIM:] + sink_terms[pp]
                o_ref[WINDOW * qb:WINDOW * (qb + 1), 512 * h + 128 * pp:512 * h + 128 * (pp + 1)] = (
                    pv[rows, :2 * HEAD_DIM] / den).astype(o_ref.dtype)


def _attention(qkv, sinks, bsz, seq, *, tq=512):
    n = qkv.shape[0]
    d_attn = N_Q_HEADS * HEAD_DIM
    d_kv2 = 2 * N_KV_HEADS * HEAD_DIM
    nt = seq // tq
    per_w = tq // WINDOW
    kv_col = d_attn // d_kv2
    return pl.pallas_call(
        functools.partial(_attn_kernel, tq=tq, slopes=_alibi_slopes()),
        grid=(bsz, nt),
        in_specs=[
            pl.BlockSpec(memory_space=pltpu.SMEM),
            pl.BlockSpec((tq, d_attn), lambda b, t: (b * nt + t, 0)),
            pl.BlockSpec((WINDOW, d_kv2),
                         lambda b, t: (b * nt * per_w + jnp.maximum(t * per_w - 1, 0), kv_col)),
            pl.BlockSpec((tq, d_kv2), lambda b, t: (b * nt + t, kv_col)),
        ],
        out_specs=pl.BlockSpec((tq, d_attn), lambda b, t: (b * nt + t, 0)),
        out_shape=jax.ShapeDtypeStruct((n, d_attn), BF16),
        scratch_shapes=[pltpu.VMEM((GROUP // 2 * WINDOW, 4 * WINDOW), BF16)],
        compiler_params=_params(2),
    )(sinks, qkv, qkv, qkv)


def _rnn_kernel(x_ref, w_ref, cw_ref, cb_ref, wg_ref, ba_ref, bi_ref, lam_ref, z_ref,
                rx_s, gy_s, xc_s, pa_s, pu_s, hp_s, tail_s, carry_s, *, tt, tn):
    t = pl.program_id(1)
    c_all = z_ref.shape[1]
    csb = c_all // RNN_SUPER
    n_slabs = c_all // LANES
    per_chunk = tn // LANES
    ng = tt // SUBLANES
    width = cw_ref.shape[0]

    @pl.when(t == 0)
    def _():
        carry_s[...] = jnp.zeros_like(carry_s)
        tail_s[...] = jnp.zeros_like(tail_s)

    xb = x_ref[...].astype(BF16)
    for c in range(c_all // tn):
        acc = jnp.dot(xb, w_ref[:, c * tn:(c + 1) * tn], preferred_element_type=F32)
        for k in range(per_chunk):
            rx_s[c * per_chunk + k] = acc[:, LANES * k:LANES * (k + 1)]
    for c in range(c_all // tn):
        ry = jnp.dot(xb, w_ref[:, c_all + c * tn:c_all + (c + 1) * tn], preferred_element_type=F32)
        gy = jax.nn.gelu(ry, approximate=True)
        for k in range(per_chunk):
            gy_s[c * per_chunk + k] = gy[:, LANES * k:LANES * (k + 1)]

    first_row = lax.broadcasted_iota(jnp.int32, (ng, LANES), 0) == 0
    for s in range(n_slabs):
        ls = slice(LANES * s, LANES * (s + 1))
        x_ph = [rx_s[s, pl.ds(j, ng, stride=SUBLANES), :] for j in range(SUBLANES)]
        x_prev = {m: jnp.where(first_row, tail_s[s, m:m + 1, :], pltpu.roll(x_ph[m], 1, axis=0))
                  for m in range(SUBLANES - (width - 1), SUBLANES)}
        for j in range(SUBLANES):
            acc = cb_ref[:, ls]
            for k in range(width):
                src = j - (width - 1) + k
                acc = acc + cw_ref[k:k + 1, ls] * (x_ph[src] if src >= 0 else x_prev[src + SUBLANES])
            xc_s[ng * j:ng * (j + 1), ls] = acc
        tail_s[s] = rx_s[s, tt - SUBLANES:tt, :]

    for sb in range(RNN_SUPER):
        cs = slice(csb * sb, csb * (sb + 1))
        xc = xc_s[:, cs]
        pre = jnp.dot(xc.astype(BF16), wg_ref[sb], preferred_element_type=F32)
        r = _sigmoid(pre[:, :csb] + ba_ref[:, cs])
        i = _sigmoid(pre[:, csb:] + bi_ref[:, cs])
        log_a = r * ((-LRU_C) * jax.nn.softplus(-lam_ref[:, cs]))
        a = jnp.exp(log_a)
        u = jnp.sqrt(-jnp.tanh(log_a) * (a * a + 1.0)) * (i * xc)
        pa = a[0:ng]
        pu = u[0:ng]
        pa_s[0:ng, cs] = pa
        pu_s[0:ng, cs] = pu
        for j in range(1, SUBLANES):
            rows = slice(ng * j, ng * (j + 1))
            pu = a[rows] * pu + u[rows]
            pa = a[rows] * pa
            pa_s[rows, cs] = pa
            pu_s[rows, cs] = pu

    last = ng * (SUBLANES - 1)

    def body(g, carry):
        hp_s[pl.ds(g, 1), :] = carry
        return pa_s[pl.ds(last + g, 1), :] * carry + pu_s[pl.ds(last + g, 1), :]

    carry_s[...] = lax.fori_loop(0, ng, body, carry_s[...])

    for s in range(n_slabs):
        ls = slice(LANES * s, LANES * (s + 1))
        hp = hp_s[:, ls]
        for j in range(SUBLANES):
            rows = slice(ng * j, ng * (j + 1))
            h = pa_s[rows, ls] * hp + pu_s[rows, ls]
            rx_s[s, pl.ds(j, ng, stride=SUBLANES), :] = h * gy_s[s, pl.ds(j, ng, stride=SUBLANES), :]
        z_ref[:, ls] = rx_s[s].astype(z_ref.dtype)


def _rnn(x, w_rxy, conv_w, conv_b, wg, ba, bi, lam, bsz, seq, *, tt=256, tn=512):
    n, d = x.shape
    c = w_rxy.shape[1] // 2
    nt = seq // tt
    row = lambda b, t: (b * nt + t, 0)
    vec = lambda a: a.reshape(1, c)
    slab = lambda rows: pltpu.VMEM((c // LANES, rows, LANES), F32)
    return pl.pallas_call(
        functools.partial(_rnn_kernel, tt=tt, tn=tn),
        grid=(bsz, nt),
        in_specs=[
            pl.BlockSpec((tt, d), row),
            _resident(w_rxy.shape),
            _resident(conv_w.shape), _resident((1, c)), _resident(wg.shape),
            _resident((1, c)), _resident((1, c)), _resident((1, c)),
        ],
        out_specs=pl.BlockSpec((tt, c), row),
        out_shape=jax.ShapeDtypeStruct((n, c), BF16),
        scratch_shapes=[
            slab(tt), slab(tt),
            pltpu.VMEM((tt, c), F32), pltpu.VMEM((tt, c), F32), pltpu.VMEM((tt, c), F32),
            pltpu.VMEM((tt // SUBLANES, c), F32),
            slab(SUBLANES),
            pltpu.VMEM((1, c), F32),
        ],
        compiler_params=_params(2),
    )(x, w_rxy, conv_w, vec(conv_b), wg, vec(ba), vec(bi), vec(lam))


def _gate_superblocks(w):
    nb, bw, _ = w.shape
    per = nb // RNN_SUPER
    w4 = w.reshape(RNN_SUPER, per, bw, bw)
    eye = jnp.eye(per, dtype=w.dtype)
    return jnp.einsum('sicd,ij->sicjd', w4, eye).reshape(RNN_SUPER, per * bw, per * bw)


def _merge_kernel(o_ref, z_ref, g_ref, wa_ref, wr_ref, m_ref, *, tn):
    o = o_ref[...]
    z = z_ref[...]
    d = m_ref.shape[1]
    for c in range(d // tn):
        cs = slice(c * tn, (c + 1) * tn)
        ya = jnp.dot(o, wa_ref[:, cs], preferred_element_type=F32)
        yr = jnp.dot(z, wr_ref[:, cs], preferred_element_type=F32)
        ga = g_ref[:, cs].astype(F32)
        gr = g_ref[:, d + c * tn:d + (c + 1) * tn].astype(F32)
        m_ref[:, cs] = (ga * ya + gr * yr).astype(m_ref.dtype)


def _merge(o, z, gates, wa, wr, *, tm=512, tn=512):
    n, d = o.shape
    return pl.pallas_call(
        functools.partial(_merge_kernel, tn=tn),
        grid=(n // tm,),
        in_specs=[
            pl.BlockSpec((tm, d), lambda i: (i, 0)),
            pl.BlockSpec((tm, z.shape[1]), lambda i: (i, 0)),
            pl.BlockSpec((tm, 2 * d), lambda i: (i, 0)),
            _resident(wa.shape), _resident(wr.shape),
        ],
        out_specs=pl.BlockSpec((tm, d), lambda i: (i, 0)),
        out_shape=jax.ShapeDtypeStruct((n, d), BF16),
        compiler_params=_params(1),
    )(o, z, gates, wa, wr)


def _out_ln_kernel(m_ref, x_ref, w_ref, g_ref, b_ref, y_ref, *, tn, alpha):
    for r in range(m_ref.shape[0] // ROW_SUB):
        rs = slice(r * ROW_SUB, (r + 1) * ROW_SUB)
        m = m_ref[rs, :]
        for c in range(w_ref.shape[1] // tn):
            cs = slice(c * tn, (c + 1) * tn)
            y_ref[rs, cs] = alpha * x_ref[rs, cs] + jnp.dot(m, w_ref[:, cs], preferred_element_type=F32)
        y_ref[rs, :] = _layer_norm(y_ref[rs, :], g_ref[...], b_ref[...])


def _out_ln(m, x, w, g, b, alpha, *, tm=1024, tn=512):
    n, d = x.shape
    return pl.pallas_call(
        functools.partial(_out_ln_kernel, tn=tn, alpha=alpha),
        grid=(n // tm,),
        in_specs=[
            pl.BlockSpec((tm, m.shape[1]), lambda i: (i, 0)),
            pl.BlockSpec((tm, d), lambda i: (i, 0)),
            _resident(w.shape), _resident((1, d)), _resident((1, d)),
        ],
        out_specs=pl.BlockSpec((tm, d), lambda i: (i, 0)),
        out_shape=jax.ShapeDtypeStruct((n, d), F32),
        compiler_params=_params(1),
    )(m, x, w, g.reshape(1, d), b.reshape(1, d))


def _ffn_kernel(x_ref, wu_ref, wg_ref, cw_ref, cb_ref, wd_ref, g_ref, b_ref, y_ref,
                xb_s, gs_s, tail_s, *, tm, alpha, blocks_per_seq):
    i = pl.program_id(0)
    c = pl.program_id(1)
    first = (i % blocks_per_seq) == 0

    @pl.when(c == 0)
    def _():
        xb_s[...] = x_ref[...].astype(BF16)
        y_ref[...] = jnp.zeros_like(y_ref)

    @pl.when(first)
    def _():
        gs_s[0:HALO] = jnp.zeros((HALO, gs_s.shape[1]), F32)

    @pl.when(jnp.logical_not(first))
    def _():
        gs_s[0:HALO] = tail_s[c]

    xb = xb_s[...]
    gs_s[HALO:HALO + tm] = jnp.dot(xb, wg_ref[...], preferred_element_type=F32)
    tail_s[c] = gs_s[tm:tm + HALO]
    up = jnp.dot(xb, wu_ref[...], preferred_element_type=F32)
    width = cw_ref.shape[0]
    gate = cb_ref[...]
    for k in range(width):
        off = HALO - (width - 1) + k
        gate = gate + cw_ref[k:k + 1, :] * gs_s[off:off + tm, :]
    hid = (jax.nn.gelu(gate, approximate=True) * up).astype(BF16)
    y_ref[...] += jnp.dot(hid, wd_ref[...], preferred_element_type=F32)

    @pl.when(c == pl.num_programs(1) - 1)
    def _():
        y_ref[...] = _layer_norm(alpha * x_ref[...] + y_ref[...], g_ref[...], b_ref[...])


def _ffn(x, wu, wg, conv_w, conv_b, wd, g, b, alpha, seq, *, tm=512, tf=1024):
    n, d = x.shape
    dff = wu.shape[1]
    return pl.pallas_call(
        functools.partial(_ffn_kernel, tm=tm, alpha=alpha, blocks_per_seq=seq // tm),
        grid=(n // tm, dff // tf),
        in_specs=[
            pl.BlockSpec((tm, d), lambda i, c: (i, 0)),
            pl.BlockSpec((d, tf), lambda i, c: (0, c)),
            pl.BlockSpec((d, tf), lambda i, c: (0, c)),
            pl.BlockSpec((conv_w.shape[0], tf), lambda i, c: (0, c)),
            pl.BlockSpec((1, tf), lambda i, c: (0, c)),
            pl.BlockSpec((tf, d), lambda i, c: (c, 0)),
            _resident((1, d)), _resident((1, d)),
        ],
        out_specs=pl.BlockSpec((tm, d), lambda i, c: (i, 0)),
        out_shape=jax.ShapeDtypeStruct((n, d), F32),
        scratch_shapes=[pltpu.VMEM((tm, d), BF16), pltpu.VMEM((HALO + tm, tf), F32),
                        pltpu.VMEM((dff // tf, HALO, tf), F32)],
        compiler_params=_params(2),
    )(x, wu, wg, conv_w, conv_b.reshape(1, dff), wd, g.reshape(1, d), b.reshape(1, d))


def kernel(x, w_in, b_gate, rnn_conv_w, rnn_conv_b, lru_wa, lru_ba, lru_wi, lru_bi, lru_lambda, attn_sinks,
           w_attn_proj, w_rnn_proj, w_out, ln1_g, ln1_b, ffn_w_up, ffn_w_gate, ffn_conv_w, ffn_conv_b,
           ffn_w_down, ln2_g, ln2_b):
    bsz, seq, d = x.shape
    depth = w_in.shape[0]
    alpha = float((2 * depth) ** 0.25)
    d_attn = N_Q_HEADS * HEAD_DIM
    d_qkv = d_attn + 2 * N_KV_HEADS * HEAD_DIM
    d_rnn = rnn_conv_w.shape[-1]

    h = x.reshape(bsz * seq, d)
    for l in range(depth):
        w_qkv, w_rxy, w_gl = _cast_bf16(w_in[l], (0, d_qkv, d_qkv + 2 * d_rnn, w_in.shape[-1]))
        qkv, hb = _qkv_proj(h, w_qkv, d_attn, HEAD_DIM ** -0.5)
        gates = _gates_proj(hb, w_gl, b_gate[l])
        o = _attention(qkv, attn_sinks[l], bsz, seq)

        wg = jnp.concatenate([_gate_superblocks(lru_wa[l]), _gate_superblocks(lru_wi[l])], axis=-1).astype(BF16)
        z = _rnn(h, w_rxy, rnn_conv_w[l], rnn_conv_b[l], wg, lru_ba[l], lru_bi[l], lru_lambda[l], bsz, seq)

        m = _merge(o, z, gates, _cast_bf16(w_attn_proj[l]), _cast_bf16(w_rnn_proj[l]))
        h = _out_ln(m, h, _cast_bf16(w_out[l]), ln1_g[l], ln1_b[l], alpha)
        h = _ffn(h, _cast_bf16(ffn_w_up[l]), _cast_bf16(ffn_w_gate[l]), ffn_conv_w[l], ffn_conv_b[l],
                 _cast_bf16(ffn_w_down[l]), ln2_g[l], ln2_b[l], alpha, seq)
    return h.reshape(bsz, seq, d)
```

```python
import functools

import numpy as np
import jax
import jax.numpy as jnp
from jax import lax
from jax.experimental import pallas as pl
from jax.experimental.pallas import tpu as pltpu

F32 = jnp.float32
BF16 = jnp.bfloat16

HEAD_DIM = 64
N_Q_HEADS = 32
N_KV_HEADS = 4
GROUP = N_Q_HEADS // N_KV_HEADS
WINDOW = 128
N_RNN_BLOCKS = 16
RNN_SUPER = 4
LRU_C = 8.0
LN_EPS = 1e-5
LOG2E = 1.4426950408889634
HALO = 16
SUBLANES = 8
LANES = 128
ROW_SUB = 512
VMEM_LIMIT = 56 * 1024 * 1024


def _params(n_axes, vmem_limit=VMEM_LIMIT):
    return pltpu.CompilerParams(dimension_semantics=("arbitrary",) * n_axes,
                                vmem_limit_bytes=vmem_limit)


def _resident(shape):
    nd = len(shape)
    return pl.BlockSpec(shape, lambda *_: (0,) * nd, pipeline_mode=pl.Buffered(1))


def _sigmoid(x):
    return 0.5 * jnp.tanh(0.5 * x) + 0.5


def _layer_norm(v, g, b):
    mu = jnp.mean(v, axis=-1, keepdims=True)
    d = v - mu
    var = jnp.mean(d * d, axis=-1, keepdims=True)
    return d * lax.rsqrt(var + LN_EPS) * g + b


def _cast_kernel(w_ref, *o_refs, splits):
    for o_ref, (lo, hi) in zip(o_refs, splits):
        o_ref[...] = w_ref[:, lo:hi].astype(o_ref.dtype)


def _cast_bf16(w, bounds=None, *, block_bytes=8 * 1024 * 1024):
    rows, cols = w.shape
    bounds = (0, cols) if bounds is None else tuple(bounds)
    splits = tuple(zip(bounds[:-1], bounds[1:]))
    tr = rows
    while tr * cols * 4 > block_bytes and tr % 32 == 0:
        tr //= 2
    out = pl.pallas_call(
        functools.partial(_cast_kernel, splits=splits),
        grid=(rows // tr,),
        in_specs=[pl.BlockSpec((tr, cols), lambda i: (i, 0))],
        out_specs=[pl.BlockSpec((tr, hi - lo), lambda i: (i, 0)) for lo, hi in splits],
        out_shape=[jax.ShapeDtypeStruct((rows, hi - lo), BF16) for lo, hi in splits],
        compiler_params=_params(1),
    )(w)
    return out[0] if len(out) == 1 else out


def _qkv_kernel(x_ref, w_ref, o_ref, xb_ref, *, tn, n_scaled, scale):
    for r in range(x_ref.shape[0] // ROW_SUB):
        rs = slice(r * ROW_SUB, (r + 1) * ROW_SUB)
        xb = x_ref[rs, :].astype(BF16)
        xb_ref[rs, :] = xb
        for c in range(w_ref.shape[1] // tn):
            cs = slice(c * tn, (c + 1) * tn)
            acc = jnp.dot(xb, w_ref[:, cs], preferred_element_type=F32)
            if c < n_scaled:
                acc = acc * scale
            o_ref[rs, cs] = acc.astype(o_ref.dtype)


def _qkv_proj(x, w, d_attn, scale, *, tm=1024, tn=512):
    n, d = x.shape
    nc = w.shape[1]
    return pl.pallas_call(
        functools.partial(_qkv_kernel, tn=tn, n_scaled=d_attn // tn, scale=scale),
        grid=(n // tm,),
        in_specs=[pl.BlockSpec((tm, d), lambda i: (i, 0)), _resident((d, nc))],
        out_specs=[pl.BlockSpec((tm, nc), lambda i: (i, 0)), pl.BlockSpec((tm, d), lambda i: (i, 0))],
        out_shape=[jax.ShapeDtypeStruct((n, nc), BF16), jax.ShapeDtypeStruct((n, d), BF16)],
        compiler_params=_params(1),
    )(x, w)


def _gates_kernel(x_ref, w_ref, b_ref, o_ref, *, tn):
    for r in range(x_ref.shape[0] // ROW_SUB):
        rs = slice(r * ROW_SUB, (r + 1) * ROW_SUB)
        xb = x_ref[rs, :]
        for c in range(w_ref.shape[1] // tn):
            cs = slice(c * tn, (c + 1) * tn)
            acc = jnp.dot(xb, w_ref[:, cs], preferred_element_type=F32) + b_ref[:, cs]
            o_ref[rs, cs] = _sigmoid(acc).astype(o_ref.dtype)


def _gates_proj(xb, w, bias, *, tm=1024, tn=512):
    n, d = xb.shape
    nc = w.shape[1]
    return pl.pallas_call(
        functools.partial(_gates_kernel, tn=tn),
        grid=(n // tm,),
        in_specs=[pl.BlockSpec((tm, d), lambda i: (i, 0)), _resident((d, nc)), _resident((1, nc))],
        out_specs=pl.BlockSpec((tm, nc), lambda i: (i, 0)),
        out_shape=jax.ShapeDtypeStruct((n, nc), BF16),
        compiler_params=_params(1),
    )(xb, w, bias.reshape(1, nc))


def _alibi_slopes():
    h = np.arange(1, N_Q_HEADS + 1, dtype=np.float32)
    return [float(s) * LOG2E for s in (2.0 ** (-8.0 * h / N_Q_HEADS)).astype(np.float32)]


def _split_halves(xp, odd, lo):
    xr = pltpu.roll(xp, HEAD_DIM, axis=1)
    zero = jnp.zeros_like(xp)
    if odd:
        return jnp.where(lo, xr, zero).astype(BF16), jnp.where(lo, zero, xp).astype(BF16)
    return jnp.where(lo, xp, zero).astype(BF16), jnp.where(lo, zero, xr).astype(BF16)


def _attn_kernel(sinks_ref, q_ref, kvp_ref, kvc_ref, o_ref, p_s, *, tq, slopes):
    t = pl.program_id(1)
    half = GROUP // 2
    qi = lax.broadcasted_iota(jnp.int32, (WINDOW, 2 * WINDOW), 0)
    kj = lax.broadcasted_iota(jnp.int32, (WINDOW, 2 * WINDOW), 1)
    dist = WINDOW + qi - kj
    valid = (dist >= 0) & (dist < WINDOW)
    before_start = (kj < WINDOW) & (t == 0)
    distf = dist.astype(F32)
    lo = lax.broadcasted_iota(jnp.int32, (1, 2 * HEAD_DIM), 1) < HEAD_DIM
    ones_lo = jnp.where(lo, 1.0, 0.0).astype(BF16)
    ones_hi = jnp.where(lo, 0.0, 1.0).astype(BF16)
    kv = jnp.concatenate([kvp_ref[...], kvc_ref[...]], axis=0)
    nkv = kv.shape[0]
    dk = N_KV_HEADS * HEAD_DIM
    for h in range(N_KV_HEADS):
        pair, odd = divmod(h, 2)
        k_lo, k_hi = _split_halves(kv[:, 128 * pair:128 * pair + 128].astype(F32), odd, lo)
        v_lo, v_hi = _split_halves(kv[:, dk + 128 * pair:dk + 128 * pair + 128].astype(F32), odd, lo)
        v_lo = jnp.concatenate([v_lo, jnp.broadcast_to(ones_lo, (nkv, 2 * HEAD_DIM))], axis=1)
        v_hi = jnp.concatenate([v_hi, jnp.broadcast_to(ones_hi, (nkv, 2 * HEAD_DIM))], axis=1)
        bias = [[jnp.where(valid, slopes[GROUP * h + 2 * pp + par] * distf, jnp.inf) for par in range(2)]
                for pp in range(half)]
        for qb in range(tq // WINDOW):
            rs = slice(WINDOW * qb, WINDOW * qb + 2 * WINDOW)
            kk = jnp.concatenate([k_lo[rs], k_hi[rs]], axis=0)
            vv = jnp.concatenate([v_lo[rs], v_hi[rs]], axis=0)
            ql = jnp.concatenate(
                [q_ref[WINDOW * qb:WINDOW * (qb + 1), 512 * h + 128 * pp:512 * h + 128 * (pp + 1)]
                 for pp in range(half)], axis=0)
            s = lax.dot_general(ql, kk, (((1,), (1,)), ((), ())), preferred_element_type=F32)
            sink_terms = []
            for pp in range(half):
                terms = []
                for par in range(2):
                    sink = sinks_ref[GROUP * h + 2 * pp + par] * LOG2E
                    sp = s[WINDOW * pp:WINDOW * (pp + 1), 2 * WINDOW * par:2 * WINDOW * (par + 1)] - bias[pp][par]
                    if qb == 0:
                        sp = jnp.where(before_start, -jnp.inf, sp)
                    m = jnp.maximum(jnp.max(sp, axis=-1, keepdims=True), sink)
                    p_s[WINDOW * pp:WINDOW * (pp + 1), 2 * WINDOW * par:2 * WINDOW * (par + 1)] = (
                        jnp.exp2(sp - m).astype(BF16))
                    terms.append(jnp.exp2(sink - m))
                sink_terms.append(jnp.where(lo, terms[0], terms[1]))
            pv = jnp.dot(p_s[...], vv, preferred_element_type=F32)
            for pp in range(half):
                rows = slice(WINDOW * pp, WINDOW * (pp + 1))
                den = pv[rows, 2 * HEAD_DIM:] + sink_terms[pp]
                o_ref[WINDOW * qb:WINDOW * (qb + 1), 512 * h + 128 * pp:512 * h + 128 * (pp + 1)] = (
                    pv[rows, :2 * HEAD_DIM] / den).astype(o_ref.dtype)


def _attention(qkv, sinks, bsz, seq, *, tq=512):
    n = qkv.shape[0]
    d_attn = N_Q_HEADS * HEAD_DIM
    d_kv2 = 2 * N_KV_HEADS * HEAD_DIM
    nt = seq // tq
    per_w = tq // WINDOW
    kv_col = d_attn // d_kv2
    return pl.pallas_call(
        functools.partial(_attn_kernel, tq=tq, slopes=_alibi_slopes()),
        grid=(bsz, nt),
        in_specs=[
            pl.BlockSpec(memory_space=pltpu.SMEM),
            pl.BlockSpec((tq, d_attn), lambda b, t: (b * nt + t, 0)),
            pl.BlockSpec((WINDOW, d_kv2),
                         lambda b, t: (b * nt * per_w + jnp.maximum(t * per_w - 1, 0), kv_col)),
            pl.BlockSpec((tq, d_kv2), lambda b, t: (b * nt + t, kv_col)),
        ],
        out_specs=pl.BlockSpec((tq, d_attn), lambda b, t: (b * nt + t, 0)),
        out_shape=jax.ShapeDtypeStruct((n, d_attn), BF16),
        scratch_shapes=[pltpu.VMEM((GROUP // 2 * WINDOW, 4 * WINDOW), BF16)],
        compiler_params=_params(2),
    )(sinks, qkv, qkv, qkv)


def _rnn_kernel(x_ref, w_ref, cw_ref, cb_ref, wg_ref, ba_ref, bi_ref, lam_ref, z_ref,
                rx_s, gy_s, xc_s, pa_s, pu_s, hp_s, tail_s, carry_s, *, tt, tn):
    t = pl.program_id(1)
    c_all = z_ref.shape[1]
    csb = c_all // RNN_SUPER
    n_slabs = c_all // LANES
    per_chunk = tn // LANES
    ng = tt // SUBLANES
    width = cw_ref.shape[0]

    @pl.when(t == 0)
    def _():
        carry_s[...] = jnp.zeros_like(carry_s)
        tail_s[...] = jnp.zeros_like(tail_s)

    xb = x_ref[...].astype(BF16)
    for c in range(c_all // tn):
        acc = jnp.dot(xb, w_ref[:, c * tn:(c + 1) * tn], preferred_element_type=F32)
        for k in range(per_chunk):
            rx_s[c * per_chunk + k] = acc[:, LANES * k:LANES * (k + 1)]
    for c in range(c_all // tn):
        ry = jnp.dot(xb, w_ref[:, c_all + c * tn:c_all + (c + 1) * tn], preferred_element_type=F32)
        gy = jax.nn.gelu(ry, approximate=True)
        for k in range(per_chunk):
            gy_s[c * per_chunk + k] = gy[:, LANES * k:LANES * (k + 1)]

    first_row = lax.broadcasted_iota(jnp.int32, (ng, LANES), 0) == 0
    for s in range(n_slabs):
        ls = slice(LANES * s, LANES * (s + 1))
        x_ph = [rx_s[s, pl.ds(j, ng, stride=SUBLANES), :] for j in range(SUBLANES)]
        x_prev = {m: jnp.where(first_row, tail_s[s, m:m + 1, :], pltpu.roll(x_ph[m], 1, axis=0))
                  for m in range(SUBLANES - (width - 1), SUBLANES)}
        for j in range(SUBLANES):
            acc = cb_ref[:, ls]
            for k in range(width):
                src = j - (width - 1) + k
                acc = acc + cw_ref[k:k + 1, ls] * (x_ph[src] if src >= 0 else x_prev[src + SUBLANES])
            xc_s[ng * j:ng * (j + 1), ls] = acc
        tail_s[s] = rx_s[s, tt - SUBLANES:tt, :]

    for sb in range(RNN_SUPER):
        cs = slice(csb * sb, csb * (sb + 1))
        xc = xc_s[:, cs]
        pre = jnp.dot(xc.astype(BF16), wg_ref[sb], preferred_element_type=F32)
        r = _sigmoid(pre[:, :csb] + ba_ref[:, cs])
        i = _sigmoid(pre[:, csb:] + bi_ref[:, cs])
        log_a = r * ((-LRU_C) * jax.nn.softplus(-lam_ref[:, cs]))
        a = jnp.exp(log_a)
        u = jnp.sqrt(-jnp.tanh(log_a) * (a * a + 1.0)) * (i * xc)
        pa = a[0:ng]
        pu = u[0:ng]
        pa_s[0:ng, cs] = pa
        pu_s[0:ng, cs] = pu
        for j in range(1, SUBLANES):
            rows = slice(ng * j, ng * (j + 1))
            pu = a[rows] * pu + u[rows]
            pa = a[rows] * pa
            pa_s[rows, cs] = pa
            pu_s[rows, cs] = pu

    last = ng * (SUBLANES - 1)

    def body(g, carry):
        hp_s[pl.ds(g, 1), :] = carry
        return pa_s[pl.ds(last + g, 1), :] * carry + pu_s[pl.ds(last + g, 1), :]

    carry_s[...] = lax.fori_loop(0, ng, body, carry_s[...], unroll=4)

    for s in range(n_slabs):
        ls = slice(LANES * s, LANES * (s + 1))
        hp = hp_s[:, ls]
        for j in range(SUBLANES):
            rows = slice(ng * j, ng * (j + 1))
            h = pa_s[rows, ls] * hp + pu_s[rows, ls]
            rx_s[s, pl.ds(j, ng, stride=SUBLANES), :] = h * gy_s[s, pl.ds(j, ng, stride=SUBLANES), :]
        z_ref[:, ls] = rx_s[s].astype(z_ref.dtype)


def _rnn(x, w_rxy, conv_w, conv_b, wg, ba, bi, lam, bsz, seq, *, tt=256, tn=512):
    n, d = x.shape
    c = w_rxy.shape[1] // 2
    nt = seq // tt
    row = lambda b, t: (b * nt + t, 0)
    vec = lambda a: a.reshape(1, c)
    slab = lambda rows: pltpu.VMEM((c // LANES, rows, LANES), F32)
    return pl.pallas_call(
        functools.partial(_rnn_kernel, tt=tt, tn=tn),
        grid=(bsz, nt),
        in_specs=[
            pl.BlockSpec((tt, d), row),
            _resident(w_rxy.shape),
            _resident(conv_w.shape), _resident((1, c)), _resident(wg.shape),
            _resident((1, c)), _resident((1, c)), _resident((1, c)),
        ],
        out_specs=pl.BlockSpec((tt, c), row),
        out_shape=jax.ShapeDtypeStruct((n, c), BF16),
        scratch_shapes=[
            slab(tt), slab(tt),
            pltpu.VMEM((tt, c), F32), pltpu.VMEM((tt, c), F32), pltpu.VMEM((tt, c), F32),
            pltpu.VMEM((tt // SUBLANES, c), F32),
            slab(SUBLANES),
            pltpu.VMEM((1, c), F32),
        ],
        compiler_params=_params(2),
    )(x, w_rxy, conv_w, vec(conv_b), wg, vec(ba), vec(bi), vec(lam))


def _superblock_kernel(wa_ref, wi_ref, o_ref):
    per, bw, _ = wa_ref.shape
    wide = per * bw
    row = lax.broadcasted_iota(jnp.int32, (bw, wide), 0)
    col = lax.broadcasted_iota(jnp.int32, (bw, wide), 1)
    for i in range(per):
        place = jnp.where(col == row + bw * i, 1.0, 0.0).astype(BF16)
        rs = slice(bw * i, bw * (i + 1))
        o_ref[0, rs, 0:wide] = jnp.dot(wa_ref[i].astype(BF16), place, preferred_element_type=F32).astype(o_ref.dtype)
        o_ref[0, rs, wide:2 * wide] = jnp.dot(wi_ref[i].astype(BF16), place,
                                              preferred_element_type=F32).astype(o_ref.dtype)


def _gate_superblocks(wa, wi):
    nb, bw, _ = wa.shape
    per = nb // RNN_SUPER
    blk = pl.BlockSpec((per, bw, bw), lambda s: (s, 0, 0))
    return pl.pallas_call(
        _superblock_kernel,
        grid=(RNN_SUPER,),
        in_specs=[blk, blk],
        out_specs=pl.BlockSpec((1, per * bw, 2 * per * bw), lambda s: (s, 0, 0)),
        out_shape=jax.ShapeDtypeStruct((RNN_SUPER, per * bw, 2 * per * bw), BF16),
        compiler_params=_params(1),
    )(wa, wi)


def _merge_kernel(o_ref, z_ref, g_ref, wa_ref, wr_ref, m_ref, *, tn):
    o = o_ref[...]
    z = z_ref[...]
    d = m_ref.shape[1]
    for c in range(d // tn):
        cs = slice(c * tn, (c + 1) * tn)
        ya = jnp.dot(o, wa_ref[:, cs], preferred_element_type=F32)
        yr = jnp.dot(z, wr_ref[:, cs], preferred_element_type=F32)
        ga = g_ref[:, cs].astype(F32)
        gr = g_ref[:, d + c * tn:d + (c + 1) * tn].astype(F32)
        m_ref[:, cs] = (ga * ya + gr * yr).astype(m_ref.dtype)


def _merge(o, z, gates, wa, wr, *, tm=512, tn=512):
    n, d = o.shape
    return pl.pallas_call(
        functools.partial(_merge_kernel, tn=tn),
        grid=(n // tm,),
        in_specs=[
            pl.BlockSpec((tm, d), lambda i: (i, 0)),
            pl.BlockSpec((tm, z.shape[1]), lambda i: (i, 0)),
            pl.BlockSpec((tm, 2 * d), lambda i: (i, 0)),
            _resident(wa.shape), _resident(wr.shape),
        ],
        out_specs=pl.BlockSpec((tm, d), lambda i: (i, 0)),
        out_shape=jax.ShapeDtypeStruct((n, d), BF16),
        compiler_params=_params(1),
    )(o, z, gates, wa, wr)


def _out_ln_kernel(m_ref, x_ref, w_ref, g_ref, b_ref, y_ref, *, tn, alpha):
    for r in range(m_ref.shape[0] // ROW_SUB):
        rs = slice(r * ROW_SUB, (r + 1) * ROW_SUB)
        m = m_ref[rs, :]
        for c in range(w_ref.shape[1] // tn):
            cs = slice(c * tn, (c + 1) * tn)
            y_ref[rs, cs] = alpha * x_ref[rs, cs] + jnp.dot(m, w_ref[:, cs], preferred_element_type=F32)
        y_ref[rs, :] = _layer_norm(y_ref[rs, :], g_ref[...], b_ref[...])


def _out_ln(m, x, w, g, b, alpha, *, tm=1024, tn=512):
    n, d = x.shape
    return pl.pallas_call(
        functools.partial(_out_ln_kernel, tn=tn, alpha=alpha),
        grid=(n // tm,),
        in_specs=[
            pl.BlockSpec((tm, m.shape[1]), lambda i: (i, 0)),
            pl.BlockSpec((tm, d), lambda i: (i, 0)),
            _resident(w.shape), _resident((1, d)), _resident((1, d)),
        ],
        out_specs=pl.BlockSpec((tm, d), lambda i: (i, 0)),
        out_shape=jax.ShapeDtypeStruct((n, d), F32),
        compiler_params=_params(1),
    )(m, x, w, g.reshape(1, d), b.reshape(1, d))


def _ffn_kernel(x_ref, wu_ref, wg_ref, cw_ref, cb_ref, wd_ref, g_ref, b_ref, y_ref,
                xb_s, gs_s, tail_s, *, tm, alpha, blocks_per_seq):
    i = pl.program_id(0)
    c = pl.program_id(1)
    first = (i % blocks_per_seq) == 0

    @pl.when(c == 0)
    def _():
        xb_s[...] = x_ref[...].astype(BF16)
        y_ref[...] = alpha * x_ref[...]

    @pl.when(first)
    def _():
        gs_s[0:HALO] = jnp.zeros((HALO, gs_s.shape[1]), F32)

    @pl.when(jnp.logical_not(first))
    def _():
        gs_s[0:HALO] = tail_s[c]

    xb = xb_s[...]
    gs_s[HALO:HALO + tm] = jnp.dot(xb, wg_ref[...], preferred_element_type=F32)
    tail_s[c] = gs_s[tm:tm + HALO]
    up = jnp.dot(xb, wu_ref[...], preferred_element_type=F32)
    width = cw_ref.shape[0]
    gate = cb_ref[...]
    for k in range(width):
        off = HALO - (width - 1) + k
        gate = gate + cw_ref[k:k + 1, :] * gs_s[off:off + tm, :]
    hid = (jax.nn.gelu(gate, approximate=True) * up).astype(BF16)
    y_ref[...] += jnp.dot(hid, wd_ref[...], preferred_element_type=F32)

    @pl.when(c == pl.num_programs(1) - 1)
    def _():
        y_ref[...] = _layer_norm(y_ref[...], g_ref[...], b_ref[...])


def _ffn(x, wu, wg, conv_w, conv_b, wd, g, b, alpha, seq, *, tm=512, tf=1024):
    n, d = x.shape
    dff = wu.shape[1]
    return pl.pallas_call(
        functools.partial(_ffn_kernel, tm=tm, alpha=alpha, blocks_per_seq=seq // tm),
        grid=(n // tm, dff // tf),
        in_specs=[
            pl.BlockSpec((tm, d), lambda i, c: (i, 0)),
            pl.BlockSpec((d, tf), lambda i, c: (0, c)),
            pl.BlockSpec((d, tf), lambda i, c: (0, c)),
            pl.BlockSpec((conv_w.shape[0], tf), lambda i, c: (0, c)),
            pl.BlockSpec((1, tf), lambda i, c: (0, c)),
            pl.BlockSpec((tf, d), lambda i, c: (c, 0)),
            _resident((1, d)), _resident((1, d)),
        ],
        out_specs=pl.BlockSpec((tm, d), lambda i, c: (i, 0)),
        out_shape=jax.ShapeDtypeStruct((n, d), F32),
        scratch_shapes=[pltpu.VMEM((tm, d), BF16), pltpu.VMEM((HALO + tm, tf), F32),
                        pltpu.VMEM((dff // tf, HALO, tf), F32)],
        compiler_params=_params(2),
    )(x, wu, wg, conv_w, conv_b.reshape(1, dff), wd, g.reshape(1, d), b.reshape(1, d))


def kernel(x, w_in, b_gate, rnn_conv_w, rnn_conv_b, lru_wa, lru_ba, lru_wi, lru_bi, lru_lambda, attn_sinks,
           w_attn_proj, w_rnn_proj, w_out, ln1_g, ln1_b, ffn_w_up, ffn_w_gate, ffn_conv_w, ffn_conv_b,
           ffn_w_down, ln2_g, ln2_b):
    bsz, seq, d = x.shape
    depth = w_in.shape[0]
    alpha = float((2 * depth) ** 0.25)
    d_attn = N_Q_HEADS * HEAD_DIM
    d_qkv = d_attn + 2 * N_KV_HEADS * HEAD_DIM
    d_rnn = rnn_conv_w.shape[-1]

    h = x.reshape(bsz * seq, d)
    for l in range(depth):
        w_qkv, w_rxy, w_gl = _cast_bf16(w_in[l], (0, d_qkv, d_qkv + 2 * d_rnn, w_in.shape[-1]))
        qkv, hb = _qkv_proj(h, w_qkv, d_attn, HEAD_DIM ** -0.5 * LOG2E)
        gates = _gates_proj(hb, w_gl, b_gate[l])
        o = _attention(qkv, attn_sinks[l], bsz, seq)

        wg = _gate_superblocks(lru_wa[l], lru_wi[l])
        z = _rnn(h, w_rxy, rnn_conv_w[l], rnn_conv_b[l], wg, lru_ba[l], lru_bi[l], lru_lambda[l], bsz, seq)

        m = _merge(o, z, gates, _cast_bf16(w_attn_proj[l]), _cast_bf16(w_rnn_proj[l]))
        h = _out_ln(m, h, _cast_bf16(w_out[l]), ln1_g[l], ln1_b[l], alpha)
        h = _ffn(h, _cast_bf16(ffn_w_up[l]), _cast_bf16(ffn_w_gate[l]), ffn_conv_w[l], ffn_conv_b[l],
                 _cast_bf16(ffn_w_down[l]), ln2_g[l], ln2_b[l], alpha, seq)
    return h.reshape(bsz, seq, d)
```

```python
import functools

import numpy as np
import jax
import jax.numpy as jnp
from jax import lax
from jax.experimental import pallas as pl
from jax.experimental.pallas import tpu as pltpu

F32 = jnp.float32
BF16 = jnp.bfloat16

HEAD_DIM = 64
N_Q_HEADS = 32
N_KV_HEADS = 4
GROUP = N_Q_HEADS // N_KV_HEADS
WINDOW = 128
N_RNN_BLOCKS = 16
RNN_SUPER = 4
LRU_C = 8.0
LN_EPS = 1e-5
LOG2E = 1.4426950408889634
GELU_C0 = 0.7978845608028654
GELU_C1 = GELU_C0 * 0.044715
HALO = 16
SUBLANES = 8
LANES = 128
COL_UNIT = 512
ROW_SUB = 512
VMEM_LIMIT = 56 * 1024 * 1024


def _params(n_axes, vmem_limit=VMEM_LIMIT):
    return pltpu.CompilerParams(dimension_semantics=("arbitrary",) * n_axes,
                                vmem_limit_bytes=vmem_limit)


def _resident(shape):
    nd = len(shape)
    return pl.BlockSpec(shape, lambda *_: (0,) * nd, pipeline_mode=pl.Buffered(1))


def _split_cols(width):
    if width % (2 * COL_UNIT):
        return [(width, 0)]
    return [(width - COL_UNIT, 0), (COL_UNIT, width - COL_UNIT)]


def _resident_split(shape):
    rows, width = shape
    return [pl.BlockSpec((rows, w), functools.partial(lambda j, *_: (0, j), off // w), pipeline_mode=pl.Buffered(1))
            for w, off in _split_cols(width)]


def _col_chunks(refs, tn):
    out, off = [], 0
    for r in refs:
        for c in range(r.shape[1] // tn):
            out.append((r, slice(c * tn, (c + 1) * tn), slice(off, off + tn)))
            off += tn
    return out


def _gelu_tanh(x):
    hx = 0.5 * x
    return hx * jnp.tanh(x * (GELU_C0 + GELU_C1 * (x * x))) + hx


def _layer_norm(v, g, b):
    mu = jnp.mean(v, axis=-1, keepdims=True)
    d = v - mu
    var = jnp.mean(d * d, axis=-1, keepdims=True)
    return d * lax.rsqrt(var + LN_EPS) * g + b


def _cast_kernel(w_ref, *o_refs, splits, scales):
    for o_ref, (lo, hi), scale in zip(o_refs, splits, scales):
        w = w_ref[:, lo:hi]
        o_ref[...] = (w if scale == 1.0 else w * scale).astype(o_ref.dtype)


def _cast_bf16(w, bounds=None, scales=None, *, block_bytes=8 * 1024 * 1024):
    rows, cols = w.shape
    bounds = (0, cols) if bounds is None else tuple(bounds)
    splits = tuple(zip(bounds[:-1], bounds[1:]))
    scales = (1.0,) * len(splits) if scales is None else tuple(scales)
    tr = rows
    while tr * cols * 4 > block_bytes and tr % 32 == 0:
        tr //= 2
    out = pl.pallas_call(
        functools.partial(_cast_kernel, splits=splits, scales=scales),
        grid=(rows // tr,),
        in_specs=[pl.BlockSpec((tr, cols), lambda i: (i, 0))],
        out_specs=[pl.BlockSpec((tr, hi - lo), lambda i: (i, 0)) for lo, hi in splits],
        out_shape=[jax.ShapeDtypeStruct((rows, hi - lo), BF16) for lo, hi in splits],
        compiler_params=_params(1),
    )(w)
    return out[0] if len(out) == 1 else out


def _qkv_kernel(x_ref, w_ref, o_ref, xb_ref, *, tn, n_scaled, scale):
    for r in range(x_ref.shape[0] // ROW_SUB):
        rs = slice(r * ROW_SUB, (r + 1) * ROW_SUB)
        xb = x_ref[rs, :].astype(BF16)
        xb_ref[rs, :] = xb
        for c in range(w_ref.shape[1] // tn):
            cs = slice(c * tn, (c + 1) * tn)
            acc = jnp.dot(xb, w_ref[:, cs], preferred_element_type=F32)
            if c < n_scaled:
                acc = acc * scale
            o_ref[rs, cs] = acc.astype(o_ref.dtype)


def _qkv_proj(x, w, d_attn, scale, *, tm=1024, tn=512):
    n, d = x.shape
    nc = w.shape[1]
    return pl.pallas_call(
        functools.partial(_qkv_kernel, tn=tn, n_scaled=d_attn // tn, scale=scale),
        grid=(n // tm,),
        in_specs=[pl.BlockSpec((tm, d), lambda i: (i, 0)), _resident((d, nc))],
        out_specs=[pl.BlockSpec((tm, nc), lambda i: (i, 0)), pl.BlockSpec((tm, d), lambda i: (i, 0))],
        out_shape=[jax.ShapeDtypeStruct((n, nc), BF16), jax.ShapeDtypeStruct((n, d), BF16)],
        compiler_params=_params(1),
    )(x, w)


def _gates_kernel(x_ref, *refs, tn):
    *w_refs, b_ref, o_ref = refs
    for r in range(x_ref.shape[0] // ROW_SUB):
        rs = slice(r * ROW_SUB, (r + 1) * ROW_SUB)
        xb = x_ref[rs, :]
        for w_ref, lc, gc in _col_chunks(w_refs, tn):
            half_z = jnp.dot(xb, w_ref[:, lc], preferred_element_type=F32) + 0.5 * b_ref[:, gc]
            o_ref[rs, gc] = (0.5 * jnp.tanh(half_z) + 0.5).astype(o_ref.dtype)


def _gates_proj(xb, w, bias, *, tm=1024, tn=COL_UNIT):
    n, d = xb.shape
    nc = w.shape[1]
    w_specs = _resident_split((d, nc))
    return pl.pallas_call(
        functools.partial(_gates_kernel, tn=tn),
        grid=(n // tm,),
        in_specs=[pl.BlockSpec((tm, d), lambda i: (i, 0)), *w_specs, _resident((1, nc))],
        out_specs=pl.BlockSpec((tm, nc), lambda i: (i, 0)),
        out_shape=jax.ShapeDtypeStruct((n, nc), BF16),
        compiler_params=_params(1),
    )(xb, *([w] * len(w_specs)), bias.reshape(1, nc))


def _alibi_slopes():
    h = np.arange(1, N_Q_HEADS + 1, dtype=np.float32)
    return [float(s) * LOG2E for s in (2.0 ** (-8.0 * h / N_Q_HEADS)).astype(np.float32)]


def _split_halves(xp, odd, lo):
    xr = pltpu.roll(xp, HEAD_DIM, axis=1)
    zero = jnp.zeros_like(xp)
    if odd:
        return jnp.where(lo, xr, zero).astype(BF16), jnp.where(lo, zero, xp).astype(BF16)
    return jnp.where(lo, xp, zero).astype(BF16), jnp.where(lo, zero, xr).astype(BF16)


def _attn_kernel(sinks_ref, q_ref, kvp_ref, kvc_ref, o_ref, p_s, *, tq, slopes):
    t = pl.program_id(1)
    half = GROUP // 2
    qi = lax.broadcasted_iota(jnp.int32, (WINDOW, 2 * WINDOW), 0)
    kj = lax.broadcasted_iota(jnp.int32, (WINDOW, 2 * WINDOW), 1)
    dist = WINDOW + qi - kj
    valid = (dist >= 0) & (dist < WINDOW)
    before_start = (kj < WINDOW) & (t == 0)
    distf = dist.astype(F32)
    lo = lax.broadcasted_iota(jnp.int32, (1, 2 * HEAD_DIM), 1) < HEAD_DIM
    ones_lo = jnp.where(lo, 1.0, 0.0).astype(BF16)
    ones_hi = jnp.where(lo, 0.0, 1.0).astype(BF16)
    kv = jnp.concatenate([kvp_ref[...], kvc_ref[...]], axis=0)
    nkv = kv.shape[0]
    dk = N_KV_HEADS * HEAD_DIM
    for h in range(N_KV_HEADS):
        pair, odd = divmod(h, 2)
        k_lo, k_hi = _split_halves(kv[:, 128 * pair:128 * pair + 128].astype(F32), odd, lo)
        v_lo, v_hi = _split_halves(kv[:, dk + 128 * pair:dk + 128 * pair + 128].astype(F32), odd, lo)
        v_lo = jnp.concatenate([v_lo, jnp.broadcast_to(ones_lo, (nkv, 2 * HEAD_DIM))], axis=1)
        v_hi = jnp.concatenate([v_hi, jnp.broadcast_to(ones_hi, (nkv, 2 * HEAD_DIM))], axis=1)
        bias = [[jnp.where(valid, slopes[GROUP * h + 2 * pp + par] * distf, jnp.inf) for par in range(2)]
                for pp in range(half)]
        for qb in range(tq // WINDOW):
            rs = slice(WINDOW * qb, WINDOW * qb + 2 * WINDOW)
            kk = jnp.concatenate([k_lo[rs], k_hi[rs]], axis=0)
            vv = jnp.concatenate([v_lo[rs], v_hi[rs]], axis=0)
            ql = jnp.concatenate(
                [q_ref[WINDOW * qb:WINDOW * (qb + 1), 512 * h + 128 * pp:512 * h + 128 * (pp + 1)]
                 for pp in range(half)], axis=0)
            s = lax.dot_general(ql, kk, (((1,), (1,)), ((), ())), preferred_element_type=F32)
            sink_terms = []
            for pp in range(half):
                terms = []
                for par in range(2):
                    sink = sinks_ref[GROUP * h + 2 * pp + par] * LOG2E
                    sp = s[WINDOW * pp:WINDOW * (pp + 1), 2 * WINDOW * par:2 * WINDOW * (par + 1)] - bias[pp][par]
                    if qb == 0:
                        sp = jnp.where(before_start, -jnp.inf, sp)
                    m = jnp.maximum(jnp.max(sp, axis=-1, keepdims=True), sink)
                    p_s[WINDOW * pp:WINDOW * (pp + 1), 2 * WINDOW * par:2 * WINDOW * (par + 1)] = (
                        jnp.exp2(sp - m).astype(BF16))
                    terms.append(jnp.exp2(sink - m))
                sink_terms.append(jnp.where(lo, terms[0], terms[1]))
            pv = jnp.dot(p_s[...], vv, preferred_element_type=F32)
            for pp in range(half):
                rows = slice(WINDOW * pp, WINDOW * (pp + 1))
                den = pv[rows, 2 * HEAD_DIM:] + sink_terms[pp]
                o_ref[WINDOW * qb:WINDOW * (qb + 1), 512 * h + 128 * pp:512 * h + 128 * (pp + 1)] = (
                    pv[rows, :2 * HEAD_DIM] / den).astype(o_ref.dtype)


def _attention(qkv, sinks, bsz, seq, *, tq=512):
    n = qkv.shape[0]
    d_attn = N_Q_HEADS * HEAD_DIM
    d_kv2 = 2 * N_KV_HEADS * HEAD_DIM
    nt = seq // tq
    per_w = tq // WINDOW
    kv_col = d_attn // d_kv2
    return pl.pallas_call(
        functools.partial(_attn_kernel, tq=tq, slopes=_alibi_slopes()),
        grid=(bsz, nt),
        in_specs=[
            pl.BlockSpec(memory_space=pltpu.SMEM),
            pl.BlockSpec((tq, d_attn), lambda b, t: (b * nt + t, 0)),
            pl.BlockSpec((WINDOW, d_kv2),
                         lambda b, t: (b * nt * per_w + jnp.maximum(t * per_w - 1, 0), kv_col)),
            pl.BlockSpec((tq, d_kv2), lambda b, t: (b * nt + t, kv_col)),
        ],
        out_specs=pl.BlockSpec((tq, d_attn), lambda b, t: (b * nt + t, 0)),
        out_shape=jax.ShapeDtypeStruct((n, d_attn), BF16),
        scratch_shapes=[pltpu.VMEM((GROUP // 2 * WINDOW, 4 * WINDOW), BF16)],
        compiler_params=_params(2),
    )(sinks, qkv, qkv, qkv)


def _rnn_kernel(x_ref, wrx_ref, wry_ref, cw_ref, cb_ref, wg_ref, ba_ref, bi_ref, lam_ref, z_ref,
                rx_s, gy_s, xc_s, pa_s, pu_s, hp_s, tail_s, carry_s, *, tt, tn):
    t = pl.program_id(1)
    c_all = z_ref.shape[1]
    csb = c_all // RNN_SUPER
    n_slabs = c_all // LANES
    per_chunk = tn // LANES
    ng = tt // SUBLANES
    width = cw_ref.shape[0]

    @pl.when(t == 0)
    def _():
        carry_s[...] = jnp.zeros_like(carry_s)
        tail_s[...] = jnp.zeros_like(tail_s)

    xb = x_ref[...].astype(BF16)
    for c in range(c_all // tn):
        acc = jnp.dot(xb, wrx_ref[:, c * tn:(c + 1) * tn], preferred_element_type=F32)
        for k in range(per_chunk):
            rx_s[c * per_chunk + k] = acc[:, LANES * k:LANES * (k + 1)]

    def ry_chunk(c):
        ry = jnp.dot(xb, wry_ref[:, c * tn:(c + 1) * tn], preferred_element_type=F32)
        gy = _gelu_tanh(ry)
        for k in range(per_chunk):
            gy_s[c * per_chunk + k] = gy[:, LANES * k:LANES * (k + 1)]

    ry_chunks = list(range(c_all // tn))

    first_row = lax.broadcasted_iota(jnp.int32, (ng, LANES), 0) == 0
    for s in range(n_slabs):
        ls = slice(LANES * s, LANES * (s + 1))
        x_ph = [rx_s[s, pl.ds(j, ng, stride=SUBLANES), :] for j in range(SUBLANES)]
        x_prev = {m: jnp.where(first_row, tail_s[s, m:m + 1, :], pltpu.roll(x_ph[m], 1, axis=0))
                  for m in range(SUBLANES - (width - 1), SUBLANES)}
        for j in range(SUBLANES):
            acc = cb_ref[:, ls]
            for k in range(width):
                src = j - (width - 1) + k
                acc = acc + cw_ref[k:k + 1, ls] * (x_ph[src] if src >= 0 else x_prev[src + SUBLANES])
            xc_s[ng * j:ng * (j + 1), ls] = acc
        tail_s[s] = rx_s[s, tt - SUBLANES:tt, :]

    for sb in range(RNN_SUPER):
        cs = slice(csb * sb, csb * (sb + 1))
        xc = xc_s[:, cs]
        pre = jnp.dot(xc.astype(BF16), wg_ref[sb], preferred_element_type=F32)
        n_now = -(-len(ry_chunks) // (RNN_SUPER - sb))
        for c in ry_chunks[:n_now]:
            ry_chunk(c)
        ry_chunks = ry_chunks[n_now:]
        ta = jnp.tanh(pre[:, :csb] + 0.5 * ba_ref[:, cs])
        ti = jnp.tanh(pre[:, csb:] + 0.5 * bi_ref[:, cs])
        half_c_sp = (0.5 * LRU_C) * jax.nn.softplus(-lam_ref[:, cs])
        neg_log_a = ta * half_c_sp + half_c_sp
        a = jnp.exp2(neg_log_a * (-LOG2E))
        hxc = 0.5 * xc
        ix = hxc * ti + hxc
        u = jnp.sqrt(jnp.tanh(neg_log_a) * (a * a + 1.0)) * ix
        pa = a[0:ng]
        pu = u[0:ng]
        pa_s[0:ng, cs] = pa
        pu_s[0:ng, cs] = pu
        for j in range(1, SUBLANES):
            rows = slice(ng * j, ng * (j + 1))
            pu = a[rows] * pu + u[rows]
            pa = a[rows] * pa
            pa_s[rows, cs] = pa
            pu_s[rows, cs] = pu

    last = ng * (SUBLANES - 1)

    def body(g, carry):
        hp_s[pl.ds(g, 1), :] = carry
        return pa_s[pl.ds(last + g, 1), :] * carry + pu_s[pl.ds(last + g, 1), :]

    carry_s[...] = lax.fori_loop(0, ng, body, carry_s[...], unroll=4)

    for s in range(n_slabs):
        ls = slice(LANES * s, LANES * (s + 1))
        hp = hp_s[:, ls]
        for j in range(SUBLANES):
            rows = slice(ng * j, ng * (j + 1))
            h = pa_s[rows, ls] * hp + pu_s[rows, ls]
            rx_s[s, pl.ds(j, ng, stride=SUBLANES), :] = h * gy_s[s, pl.ds(j, ng, stride=SUBLANES), :]
        z_ref[:, ls] = rx_s[s].astype(z_ref.dtype)


def _rnn(x, w_rxy, conv_w, conv_b, wg, ba, bi, lam, bsz, seq, *, tt=256, tn=512):
    n, d = x.shape
    c = w_rxy.shape[1] // 2
    nt = seq // tt
    row = lambda b, t: (b * nt + t, 0)
    vec = lambda a: a.reshape(1, c)
    slab = lambda rows: pltpu.VMEM((c // LANES, rows, LANES), F32)
    return pl.pallas_call(
        functools.partial(_rnn_kernel, tt=tt, tn=tn),
        grid=(bsz, nt),
        in_specs=[
            pl.BlockSpec((tt, d), row),
            pl.BlockSpec((d, c), lambda b, t: (0, 0), pipeline_mode=pl.Buffered(1)),
            pl.BlockSpec((d, c), lambda b, t: (0, 1), pipeline_mode=pl.Buffered(1)),
            _resident(conv_w.shape), _resident((1, c)), _resident(wg.shape),
            _resident((1, c)), _resident((1, c)), _resident((1, c)),
        ],
        out_specs=pl.BlockSpec((tt, c), row),
        out_shape=jax.ShapeDtypeStruct((n, c), BF16),
        scratch_shapes=[
            slab(tt), slab(tt),
            pltpu.VMEM((tt, c), F32), pltpu.VMEM((tt, c), F32), pltpu.VMEM((tt, c), F32),
            pltpu.VMEM((tt // SUBLANES, c), F32),
            slab(SUBLANES),
            pltpu.VMEM((1, c), F32),
        ],
        compiler_params=_params(2),
    )(x, w_rxy, w_rxy, conv_w, vec(conv_b), wg, vec(ba), vec(bi), vec(lam))


def _superblock_kernel(wa_ref, wi_ref, o_ref):
    per, bw, _ = wa_ref.shape
    wide = per * bw
    row = lax.broadcasted_iota(jnp.int32, (bw, wide), 0)
    col = lax.broadcasted_iota(jnp.int32, (bw, wide), 1)
    for i in range(per):
        place = jnp.where(col == row + bw * i, 0.5, 0.0).astype(BF16)
        rs = slice(bw * i, bw * (i + 1))
        o_ref[0, rs, 0:wide] = jnp.dot(wa_ref[i].astype(BF16), place, preferred_element_type=F32).astype(o_ref.dtype)
        o_ref[0, rs, wide:2 * wide] = jnp.dot(wi_ref[i].astype(BF16), place,
                                              preferred_element_type=F32).astype(o_ref.dtype)


def _gate_superblocks(wa, wi):
    nb, bw, _ = wa.shape
    per = nb // RNN_SUPER
    blk = pl.BlockSpec((per, bw, bw), lambda s: (s, 0, 0))
    return pl.pallas_call(
        _superblock_kernel,
        grid=(RNN_SUPER,),
        in_specs=[blk, blk],
        out_specs=pl.BlockSpec((1, per * bw, 2 * per * bw), lambda s: (s, 0, 0)),
        out_shape=jax.ShapeDtypeStruct((RNN_SUPER, per * bw, 2 * per * bw), BF16),
        compiler_params=_params(1),
    )(wa, wi)


def _merge_kernel(o_ref, z_ref, g_ref, *refs, tn, n_split):
    wa_refs, wr_refs, m_ref = refs[:n_split], refs[n_split:2 * n_split], refs[2 * n_split]
    o = o_ref[...]
    z = z_ref[...]
    d = m_ref.shape[1]
    for (wa_ref, lc, gc), (wr_ref, _, _) in zip(_col_chunks(wa_refs, tn), _col_chunks(wr_refs, tn)):
        ya = jnp.dot(o, wa_ref[:, lc], preferred_element_type=F32)
        yr = jnp.dot(z, wr_ref[:, lc], preferred_element_type=F32)
        ga = g_ref[:, gc].astype(F32)
        gr = g_ref[:, d + gc.start:d + gc.stop].astype(F32)
        m_ref[:, gc] = (ga * ya + gr * yr).astype(m_ref.dtype)


def _merge(o, z, gates, wa, wr, *, tm=512, tn=COL_UNIT):
    n, d = o.shape
    wa_specs, wr_specs = _resident_split(wa.shape), _resident_split(wr.shape)
    assert len(wa_specs) == len(wr_specs)
    return pl.pallas_call(
        functools.partial(_merge_kernel, tn=tn, n_split=len(wa_specs)),
        grid=(n // tm,),
        in_specs=[
            pl.BlockSpec((tm, d), lambda i: (i, 0)),
            pl.BlockSpec((tm, z.shape[1]), lambda i: (i, 0)),
            pl.BlockSpec((tm, 2 * d), lambda i: (i, 0)),
            *wa_specs, *wr_specs,
        ],
        out_specs=pl.BlockSpec((tm, d), lambda i: (i, 0)),
        out_shape=jax.ShapeDtypeStruct((n, d), BF16),
        compiler_params=_params(1),
    )(o, z, gates, *([wa] * len(wa_specs)), *([wr] * len(wr_specs)))


def _out_ln_kernel(m_ref, x_ref, *refs, tn, alpha):
    *w_refs, g_ref, b_ref, y_ref = refs
    for r in range(m_ref.shape[0] // ROW_SUB):
        rs = slice(r * ROW_SUB, (r + 1) * ROW_SUB)
        m = m_ref[rs, :]
        for w_ref, lc, gc in _col_chunks(w_refs, tn):
            y_ref[rs, gc] = alpha * x_ref[rs, gc] + jnp.dot(m, w_ref[:, lc], preferred_element_type=F32)
        y_ref[rs, :] = _layer_norm(y_ref[rs, :], g_ref[...], b_ref[...])


def _out_ln(m, x, w, g, b, alpha, *, tm=1024, tn=COL_UNIT):
    n, d = x.shape
    w_specs = _resident_split(w.shape)
    return pl.pallas_call(
        functools.partial(_out_ln_kernel, tn=tn, alpha=alpha),
        grid=(n // tm,),
        in_specs=[
            pl.BlockSpec((tm, m.shape[1]), lambda i: (i, 0)),
            pl.BlockSpec((tm, d), lambda i: (i, 0)),
            *w_specs, _resident((1, d)), _resident((1, d)),
        ],
        out_specs=pl.BlockSpec((tm, d), lambda i: (i, 0)),
        out_shape=jax.ShapeDtypeStruct((n, d), F32),
        compiler_params=_params(1),
    )(m, x, *([w] * len(w_specs)), g.reshape(1, d), b.reshape(1, d))


def _ffn_kernel(x_ref, wu_ref, wg_ref, cw_ref, cb_ref, wd_ref, g_ref, b_ref, y_ref,
                xb_s, gs_s, tail_s, *, tm, alpha, blocks_per_seq):
    i = pl.program_id(0)
    c = pl.program_id(1)
    first = (i % blocks_per_seq) == 0

    @pl.when(c == 0)
    def _():
        xb_s[...] = x_ref[...].astype(BF16)
        y_ref[...] = alpha * x_ref[...]

    @pl.when(first)
    def _():
        gs_s[0:HALO] = jnp.zeros((HALO, gs_s.shape[1]), F32)

    @pl.when(jnp.logical_not(first))
    def _():
        gs_s[0:HALO] = tail_s[c]

    xb = xb_s[...]
    gs_s[HALO:HALO + tm] = jnp.dot(xb, wg_ref[...], preferred_element_type=F32)
    tail_s[c] = gs_s[tm:tm + HALO]
    up = jnp.dot(xb, wu_ref[...], preferred_element_type=F32)
    width = cw_ref.shape[0]
    gate = cb_ref[...]
    for k in range(width):
        off = HALO - (width - 1) + k
        gate = gate + cw_ref[k:k + 1, :] * gs_s[off:off + tm, :]
    hid = (jax.nn.gelu(gate, approximate=True) * up).astype(BF16)
    y_ref[...] += jnp.dot(hid, wd_ref[...], preferred_element_type=F32)

    @pl.when(c == pl.num_programs(1) - 1)
    def _():
        y_ref[...] = _layer_norm(y_ref[...], g_ref[...], b_ref[...])


def _ffn(x, wu, wg, conv_w, conv_b, wd, g, b, alpha, seq, *, tm=512, tf=1024):
    n, d = x.shape
    dff = wu.shape[1]
    return pl.pallas_call(
        functools.partial(_ffn_kernel, tm=tm, alpha=alpha, blocks_per_seq=seq // tm),
        grid=(n // tm, dff // tf),
        in_specs=[
            pl.BlockSpec((tm, d), lambda i, c: (i, 0)),
            pl.BlockSpec((d, tf), lambda i, c: (0, c)),
            pl.BlockSpec((d, tf), lambda i, c: (0, c)),
            pl.BlockSpec((conv_w.shape[0], tf), lambda i, c: (0, c)),
            pl.BlockSpec((1, tf), lambda i, c: (0, c)),
            pl.BlockSpec((tf, d), lambda i, c: (c, 0)),
            _resident((1, d)), _resident((1, d)),
        ],
        out_specs=pl.BlockSpec((tm, d), lambda i, c: (i, 0)),
        out_shape=jax.ShapeDtypeStruct((n, d), F32),
        scratch_shapes=[pltpu.VMEM((tm, d), BF16), pltpu.VMEM((HALO + tm, tf), F32),
                        pltpu.VMEM((dff // tf, HALO, tf), F32)],
        compiler_params=_params(2),
    )(x, wu, wg, conv_w, conv_b.reshape(1, dff), wd, g.reshape(1, d), b.reshape(1, d))


def kernel(x, w_in, b_gate, rnn_conv_w, rnn_conv_b, lru_wa, lru_ba, lru_wi, lru_bi, lru_lambda, attn_sinks,
           w_attn_proj, w_rnn_proj, w_out, ln1_g, ln1_b, ffn_w_up, ffn_w_gate, ffn_conv_w, ffn_conv_b,
           ffn_w_down, ln2_g, ln2_b):
    bsz, seq, d = x.shape
    depth = w_in.shape[0]
    alpha = float((2 * depth) ** 0.25)
    d_attn = N_Q_HEADS * HEAD_DIM
    d_qkv = d_attn + 2 * N_KV_HEADS * HEAD_DIM
    d_rnn = rnn_conv_w.shape[-1]

    h = x.reshape(bsz * seq, d)
    for l in range(depth):
        w_qkv, w_rxy, w_gl_half = _cast_bf16(w_in[l], (0, d_qkv, d_qkv + 2 * d_rnn, w_in.shape[-1]),
                                             (1.0, 1.0, 0.5))
        qkv, hb = _qkv_proj(h, w_qkv, d_attn, HEAD_DIM ** -0.5 * LOG2E)
        gates = _gates_proj(hb, w_gl_half, b_gate[l])
        o = _attention(qkv, attn_sinks[l], bsz, seq)

        wg = _gate_superblocks(lru_wa[l], lru_wi[l])
        z = _rnn(h, w_rxy, rnn_conv_w[l], rnn_conv_b[l], wg, lru_ba[l], lru_bi[l], lru_lambda[l], bsz, seq)

        m = _merge(o, z, gates, _cast_bf16(w_attn_proj[l]), _cast_bf16(w_rnn_proj[l]))
        h = _out_ln(m, h, _cast_bf16(w_out[l]), ln1_g[l], ln1_b[l], alpha)
        h = _ffn(h, _cast_bf16(ffn_w_up[l]), _cast_bf16(ffn_w_gate[l]), ffn_conv_w[l], ffn_conv_b[l],
                 _cast_bf16(ffn_w_down[l]), ln2_g[l], ln2_b[l], alpha, seq)
    return h.reshape(bsz, seq, d)
```

```python
import functools

import numpy as np
import jax
import jax.numpy as jnp
from jax import lax
from jax.experimental import pallas as pl
from jax.experimental.pallas import tpu as pltpu

F32 = jnp.float32
BF16 = jnp.bfloat16

HEAD_DIM = 64
N_Q_HEADS = 32
N_KV_HEADS = 4
GROUP = N_Q_HEADS // N_KV_HEADS
WINDOW = 128
N_RNN_BLOCKS = 16
RNN_SUPER = 4
LRU_C = 8.0
LN_EPS = 1e-5
LOG2E = 1.4426950408889634
GELU_C0 = 0.7978845608028654
GELU_C1 = GELU_C0 * 0.044715
HALO = 16
SUBLANES = 8
LANES = 128
COL_UNIT = 512
ROW_SUB = 512
VMEM_LIMIT = 56 * 1024 * 1024


def _params(n_axes, vmem_limit=VMEM_LIMIT):
    return pltpu.CompilerParams(dimension_semantics=("arbitrary",) * n_axes,
                                vmem_limit_bytes=vmem_limit)


def _resident(shape):
    nd = len(shape)
    return pl.BlockSpec(shape, lambda *_: (0,) * nd, pipeline_mode=pl.Buffered(1))


def _split_cols(width):
    if width % (2 * COL_UNIT):
        return [(width, 0)]
    return [(width - COL_UNIT, 0), (COL_UNIT, width - COL_UNIT)]


def _resident_split(shape):
    rows, width = shape
    return [pl.BlockSpec((rows, w), functools.partial(lambda j, *_: (0, j), off // w), pipeline_mode=pl.Buffered(1))
            for w, off in _split_cols(width)]


def _col_chunks(refs, tn):
    out, off = [], 0
    for r in refs:
        for c in range(r.shape[1] // tn):
            out.append((r, slice(c * tn, (c + 1) * tn), slice(off, off + tn)))
            off += tn
    return out


def _gelu_tanh(x):
    hx = 0.5 * x
    return hx * jnp.tanh(x * (GELU_C0 + GELU_C1 * (x * x))) + hx


def _layer_norm(v, g, b):
    mu = jnp.mean(v, axis=-1, keepdims=True)
    d = v - mu
    var = jnp.mean(d * d, axis=-1, keepdims=True)
    return d * lax.rsqrt(var + LN_EPS) * g + b


def _cast_kernel(w_ref, *o_refs, splits, scales):
    for o_ref, (lo, hi), scale in zip(o_refs, splits, scales):
        w = w_ref[:, lo:hi]
        o_ref[...] = (w if scale == 1.0 else w * scale).astype(o_ref.dtype)


def _cast_bf16(w, bounds=None, scales=None, *, block_bytes=8 * 1024 * 1024):
    rows, cols = w.shape
    bounds = (0, cols) if bounds is None else tuple(bounds)
    splits = tuple(zip(bounds[:-1], bounds[1:]))
    scales = (1.0,) * len(splits) if scales is None else tuple(scales)
    tr = rows
    while tr * cols * 4 > block_bytes and tr % 32 == 0:
        tr //= 2
    out = pl.pallas_call(
        functools.partial(_cast_kernel, splits=splits, scales=scales),
        grid=(rows // tr,),
        in_specs=[pl.BlockSpec((tr, cols), lambda i: (i, 0))],
        out_specs=[pl.BlockSpec((tr, hi - lo), lambda i: (i, 0)) for lo, hi in splits],
        out_shape=[jax.ShapeDtypeStruct((rows, hi - lo), BF16) for lo, hi in splits],
        compiler_params=_params(1),
    )(w)
    return out[0] if len(out) == 1 else out


def _qkv_kernel(x_ref, w_ref, o_ref, xb_ref, *, tn, n_scaled, scale):
    for r in range(x_ref.shape[0] // ROW_SUB):
        rs = slice(r * ROW_SUB, (r + 1) * ROW_SUB)
        xb = x_ref[rs, :].astype(BF16)
        xb_ref[rs, :] = xb
        for c in range(w_ref.shape[1] // tn):
            cs = slice(c * tn, (c + 1) * tn)
            acc = jnp.dot(xb, w_ref[:, cs], preferred_element_type=F32)
            if c < n_scaled:
                acc = acc * scale
            o_ref[rs, cs] = acc.astype(o_ref.dtype)


def _qkv_proj(x, w, d_attn, scale, *, tm=1024, tn=512):
    n, d = x.shape
    nc = w.shape[1]
    return pl.pallas_call(
        functools.partial(_qkv_kernel, tn=tn, n_scaled=d_attn // tn, scale=scale),
        grid=(n // tm,),
        in_specs=[pl.BlockSpec((tm, d), lambda i: (i, 0)), _resident((d, nc))],
        out_specs=[pl.BlockSpec((tm, nc), lambda i: (i, 0)), pl.BlockSpec((tm, d), lambda i: (i, 0))],
        out_shape=[jax.ShapeDtypeStruct((n, nc), BF16), jax.ShapeDtypeStruct((n, d), BF16)],
        compiler_params=_params(1),
    )(x, w)


def _gates_kernel(x_ref, *refs, tn):
    *w_refs, b_ref, o_ref = refs
    for r in range(x_ref.shape[0] // ROW_SUB):
        rs = slice(r * ROW_SUB, (r + 1) * ROW_SUB)
        xb = x_ref[rs, :]
        for w_ref, lc, gc in _col_chunks(w_refs, tn):
            half_z = jnp.dot(xb, w_ref[:, lc], preferred_element_type=F32) + 0.5 * b_ref[:, gc]
            o_ref[rs, gc] = (0.5 * jnp.tanh(half_z) + 0.5).astype(o_ref.dtype)


def _gates_proj(xb, w, bias, *, tm=1024, tn=COL_UNIT):
    n, d = xb.shape
    nc = w.shape[1]
    w_specs = _resident_split((d, nc))
    return pl.pallas_call(
        functools.partial(_gates_kernel, tn=tn),
        grid=(n // tm,),
        in_specs=[pl.BlockSpec((tm, d), lambda i: (i, 0)), *w_specs, _resident((1, nc))],
        out_specs=pl.BlockSpec((tm, nc), lambda i: (i, 0)),
        out_shape=jax.ShapeDtypeStruct((n, nc), BF16),
        compiler_params=_params(1),
    )(xb, *([w] * len(w_specs)), bias.reshape(1, nc))


def _alibi_slopes():
    h = np.arange(1, N_Q_HEADS + 1, dtype=np.float32)
    return [float(s) * LOG2E for s in (2.0 ** (-8.0 * h / N_Q_HEADS)).astype(np.float32)]


def _split_halves(xp, odd, lo):
    xr = pltpu.roll(xp, HEAD_DIM, axis=1)
    zero = jnp.zeros_like(xp)
    if odd:
        return jnp.where(lo, xr, zero).astype(BF16), jnp.where(lo, zero, xp).astype(BF16)
    return jnp.where(lo, xp, zero).astype(BF16), jnp.where(lo, zero, xr).astype(BF16)


def _attn_kernel(sinks_ref, q_ref, kvp_ref, kvc_ref, o_ref, p_s, *, tq, slopes):
    t = pl.program_id(1)
    half = GROUP // 2
    qi = lax.broadcasted_iota(jnp.int32, (WINDOW, 2 * WINDOW), 0)
    kj = lax.broadcasted_iota(jnp.int32, (WINDOW, 2 * WINDOW), 1)
    dist = WINDOW + qi - kj
    valid = (dist >= 0) & (dist < WINDOW)
    before_start = (kj < WINDOW) & (t == 0)
    distf = dist.astype(F32)
    lo = lax.broadcasted_iota(jnp.int32, (1, 2 * HEAD_DIM), 1) < HEAD_DIM
    ones_lo = jnp.where(lo, 1.0, 0.0).astype(BF16)
    ones_hi = jnp.where(lo, 0.0, 1.0).astype(BF16)
    kv = jnp.concatenate([kvp_ref[...], kvc_ref[...]], axis=0)
    nkv = kv.shape[0]
    dk = N_KV_HEADS * HEAD_DIM
    for h in range(N_KV_HEADS):
        pair, odd = divmod(h, 2)
        k_lo, k_hi = _split_halves(kv[:, 128 * pair:128 * pair + 128].astype(F32), odd, lo)
        v_lo, v_hi = _split_halves(kv[:, dk + 128 * pair:dk + 128 * pair + 128].astype(F32), odd, lo)
        v_lo = jnp.concatenate([v_lo, jnp.broadcast_to(ones_lo, (nkv, 2 * HEAD_DIM))], axis=1)
        v_hi = jnp.concatenate([v_hi, jnp.broadcast_to(ones_hi, (nkv, 2 * HEAD_DIM))], axis=1)
        bias = [[jnp.where(valid, slopes[GROUP * h + 2 * pp + par] * distf, jnp.inf) for par in range(2)]
                for pp in range(half)]
        for qb in range(tq // WINDOW):
            rs = slice(WINDOW * qb, WINDOW * qb + 2 * WINDOW)
            kk = jnp.concatenate([k_lo[rs], k_hi[rs]], axis=0)
            vv = jnp.concatenate([v_lo[rs], v_hi[rs]], axis=0)
            ql = jnp.concatenate(
                [q_ref[WINDOW * qb:WINDOW * (qb + 1), 512 * h + 128 * pp:512 * h + 128 * (pp + 1)]
                 for pp in range(half)], axis=0)
            s = lax.dot_general(ql, kk, (((1,), (1,)), ((), ())), preferred_element_type=F32)
            sink_terms = []
            for pp in range(half):
                terms = []
                for par in range(2):
                    sink = sinks_ref[GROUP * h + 2 * pp + par] * LOG2E
                    sp = s[WINDOW * pp:WINDOW * (pp + 1), 2 * WINDOW * par:2 * WINDOW * (par + 1)] - bias[pp][par]
                    if qb == 0:
                        sp = jnp.where(before_start, -jnp.inf, sp)
                    m = jnp.maximum(jnp.max(sp, axis=-1, keepdims=True), sink)
                    p_s[WINDOW * pp:WINDOW * (pp + 1), 2 * WINDOW * par:2 * WINDOW * (par + 1)] = (
                        jnp.exp2(sp - m).astype(BF16))
                    terms.append(jnp.exp2(sink - m))
                sink_terms.append(jnp.where(lo, terms[0], terms[1]))
            pv = jnp.dot(p_s[...], vv, preferred_element_type=F32)
            for pp in range(half):
                rows = slice(WINDOW * pp, WINDOW * (pp + 1))
                den = pv[rows, 2 * HEAD_DIM:] + sink_terms[pp]
                o_ref[WINDOW * qb:WINDOW * (qb + 1), 512 * h + 128 * pp:512 * h + 128 * (pp + 1)] = (
                    pv[rows, :2 * HEAD_DIM] / den).astype(o_ref.dtype)


def _attention(qkv, sinks, bsz, seq, *, tq=512):
    n = qkv.shape[0]
    d_attn = N_Q_HEADS * HEAD_DIM
    d_kv2 = 2 * N_KV_HEADS * HEAD_DIM
    nt = seq // tq
    per_w = tq // WINDOW
    kv_col = d_attn // d_kv2
    return pl.pallas_call(
        functools.partial(_attn_kernel, tq=tq, slopes=_alibi_slopes()),
        grid=(bsz, nt),
        in_specs=[
            pl.BlockSpec(memory_space=pltpu.SMEM),
            pl.BlockSpec((tq, d_attn), lambda b, t: (b * nt + t, 0)),
            pl.BlockSpec((WINDOW, d_kv2),
                         lambda b, t: (b * nt * per_w + jnp.maximum(t * per_w - 1, 0), kv_col)),
            pl.BlockSpec((tq, d_kv2), lambda b, t: (b * nt + t, kv_col)),
        ],
        out_specs=pl.BlockSpec((tq, d_attn), lambda b, t: (b * nt + t, 0)),
        out_shape=jax.ShapeDtypeStruct((n, d_attn), BF16),
        scratch_shapes=[pltpu.VMEM((GROUP // 2 * WINDOW, 4 * WINDOW), BF16)],
        compiler_params=_params(2),
    )(sinks, qkv, qkv, qkv)


def _rnn_kernel(x_ref, wrx_ref, wry_ref, cw_ref, cb_ref, wg_ref, ba_ref, bi_ref, lam_ref, z_ref,
                rx_s, gy_s, xc_s, pa_s, pu_s, hp_s, tail_s, carry_s, *, tt, tn):
    t = pl.program_id(1)
    c_all = z_ref.shape[1]
    csb = c_all // RNN_SUPER
    n_slabs = c_all // LANES
    per_chunk = tn // LANES
    ng = tt // SUBLANES
    width = cw_ref.shape[0]

    @pl.when(t == 0)
    def _():
        carry_s[...] = jnp.zeros_like(carry_s)
        tail_s[...] = jnp.zeros_like(tail_s)

    xb = x_ref[...].astype(BF16)
    for c in range(c_all // tn):
        acc = jnp.dot(xb, wrx_ref[:, c * tn:(c + 1) * tn], preferred_element_type=F32)
        for k in range(per_chunk):
            rx_s[c * per_chunk + k] = acc[:, LANES * k:LANES * (k + 1)]

    def ry_chunk(c):
        ry = jnp.dot(xb, wry_ref[:, c * tn:(c + 1) * tn], preferred_element_type=F32)
        gy = _gelu_tanh(ry)
        for k in range(per_chunk):
            gy_s[c * per_chunk + k] = gy[:, LANES * k:LANES * (k + 1)]

    ry_chunks = list(range(c_all // tn))

    first_row = lax.broadcasted_iota(jnp.int32, (ng, LANES), 0) == 0
    for s in range(n_slabs):
        ls = slice(LANES * s, LANES * (s + 1))
        x_ph = [rx_s[s, pl.ds(j, ng, stride=SUBLANES), :] for j in range(SUBLANES)]
        x_prev = {m: jnp.where(first_row, tail_s[s, m:m + 1, :], pltpu.roll(x_ph[m], 1, axis=0))
                  for m in range(SUBLANES - (width - 1), SUBLANES)}
        for j in range(SUBLANES):
            acc = cb_ref[:, ls]
            for k in range(width):
                src = j - (width - 1) + k
                acc = acc + cw_ref[k:k + 1, ls] * (x_ph[src] if src >= 0 else x_prev[src + SUBLANES])
            xc_s[ng * j:ng * (j + 1), ls] = acc
        tail_s[s] = rx_s[s, tt - SUBLANES:tt, :]

    for sb in range(RNN_SUPER):
        cs = slice(csb * sb, csb * (sb + 1))
        xc = xc_s[:, cs]
        pre = jnp.dot(xc.astype(BF16), wg_ref[sb], preferred_element_type=F32)
        n_now = -(-len(ry_chunks) // (RNN_SUPER - sb))
        for c in ry_chunks[:n_now]:
            ry_chunk(c)
        ry_chunks = ry_chunks[n_now:]
        ta = jnp.tanh(pre[:, :csb] + 0.5 * ba_ref[:, cs])
        ti = jnp.tanh(pre[:, csb:] + 0.5 * bi_ref[:, cs])
        half_c_sp = (0.5 * LRU_C) * jax.nn.softplus(-lam_ref[:, cs])
        neg_log_a = ta * half_c_sp + half_c_sp
        a = jnp.exp2(neg_log_a * (-LOG2E))
        hxc = 0.5 * xc
        ix = hxc * ti + hxc
        u = jnp.sqrt(jnp.tanh(neg_log_a) * (a * a + 1.0)) * ix
        pa = a[0:ng]
        pu = u[0:ng]
        pa_s[0:ng, cs] = pa
        pu_s[0:ng, cs] = pu
        for j in range(1, SUBLANES):
            rows = slice(ng * j, ng * (j + 1))
            pu = a[rows] * pu + u[rows]
            pa = a[rows] * pa
            pa_s[rows, cs] = pa
            pu_s[rows, cs] = pu

    last = ng * (SUBLANES - 1)

    def body(g, carry):
        hp_s[pl.ds(g, 1), :] = carry
        return pa_s[pl.ds(last + g, 1), :] * carry + pu_s[pl.ds(last + g, 1), :]

    carry_s[...] = lax.fori_loop(0, ng, body, carry_s[...], unroll=4)

    for s in range(n_slabs):
        ls = slice(LANES * s, LANES * (s + 1))
        hp = hp_s[:, ls]
        for j in range(SUBLANES):
            rows = slice(ng * j, ng * (j + 1))
            h = pa_s[rows, ls] * hp + pu_s[rows, ls]
            rx_s[s, pl.ds(j, ng, stride=SUBLANES), :] = h * gy_s[s, pl.ds(j, ng, stride=SUBLANES), :]
        z_ref[:, ls] = rx_s[s].astype(z_ref.dtype)


def _rnn(x, w_rxy, conv_w, conv_b, wg, ba, bi, lam, bsz, seq, *, tt=256, tn=512):
    n, d = x.shape
    c = w_rxy.shape[1] // 2
    nt = seq // tt
    row = lambda b, t: (b * nt + t, 0)
    vec = lambda a: a.reshape(1, c)
    slab = lambda rows: pltpu.VMEM((c // LANES, rows, LANES), F32)
    return pl.pallas_call(
        functools.partial(_rnn_kernel, tt=tt, tn=tn),
        grid=(bsz, nt),
        in_specs=[
            pl.BlockSpec((tt, d), row),
            pl.BlockSpec((d, c), lambda b, t: (0, 0), pipeline_mode=pl.Buffered(1)),
            pl.BlockSpec((d, c), lambda b, t: (0, 1), pipeline_mode=pl.Buffered(1)),
            _resident(conv_w.shape), _resident((1, c)), _resident(wg.shape),
            _resident((1, c)), _resident((1, c)), _resident((1, c)),
        ],
        out_specs=pl.BlockSpec((tt, c), row),
        out_shape=jax.ShapeDtypeStruct((n, c), BF16),
        scratch_shapes=[
            slab(tt), slab(tt),
            pltpu.VMEM((tt, c), F32), pltpu.VMEM((tt, c), F32), pltpu.VMEM((tt, c), F32),
            pltpu.VMEM((tt // SUBLANES, c), F32),
            slab(SUBLANES),
            pltpu.VMEM((1, c), F32),
        ],
        compiler_params=_params(2),
    )(x, w_rxy, w_rxy, conv_w, vec(conv_b), wg, vec(ba), vec(bi), vec(lam))


def _superblock_kernel(wa_ref, wi_ref, o_ref):
    per, bw, _ = wa_ref.shape
    wide = per * bw
    row = lax.broadcasted_iota(jnp.int32, (bw, wide), 0)
    col = lax.broadcasted_iota(jnp.int32, (bw, wide), 1)
    for i in range(per):
        place = jnp.where(col == row + bw * i, 0.5, 0.0).astype(BF16)
        rs = slice(bw * i, bw * (i + 1))
        o_ref[0, rs, 0:wide] = jnp.dot(wa_ref[i].astype(BF16), place, preferred_element_type=F32).astype(o_ref.dtype)
        o_ref[0, rs, wide:2 * wide] = jnp.dot(wi_ref[i].astype(BF16), place,
                                              preferred_element_type=F32).astype(o_ref.dtype)


def _gate_superblocks(wa, wi):
    nb, bw, _ = wa.shape
    per = nb // RNN_SUPER
    blk = pl.BlockSpec((per, bw, bw), lambda s: (s, 0, 0))
    return pl.pallas_call(
        _superblock_kernel,
        grid=(RNN_SUPER,),
        in_specs=[blk, blk],
        out_specs=pl.BlockSpec((1, per * bw, 2 * per * bw), lambda s: (s, 0, 0)),
        out_shape=jax.ShapeDtypeStruct((RNN_SUPER, per * bw, 2 * per * bw), BF16),
        compiler_params=_params(1),
    )(wa, wi)


def _merge_kernel(o_ref, z_ref, g_ref, *refs, tn, n_split):
    wa_refs, wr_refs, m_ref = refs[:n_split], refs[n_split:2 * n_split], refs[2 * n_split]
    o = o_ref[...]
    z = z_ref[...]
    d = m_ref.shape[1]
    for (wa_ref, lc, gc), (wr_ref, _, _) in zip(_col_chunks(wa_refs, tn), _col_chunks(wr_refs, tn)):
        ya = jnp.dot(o, wa_ref[:, lc], preferred_element_type=F32)
        yr = jnp.dot(z, wr_ref[:, lc], preferred_element_type=F32)
        ga = g_ref[:, gc].astype(F32)
        gr = g_ref[:, d + gc.start:d + gc.stop].astype(F32)
        m_ref[:, gc] = (ga * ya + gr * yr).astype(m_ref.dtype)


def _merge(o, z, gates, wa, wr, *, tm=512, tn=COL_UNIT):
    n, d = o.shape
    wa_specs, wr_specs = _resident_split(wa.shape), _resident_split(wr.shape)
    assert len(wa_specs) == len(wr_specs)
    return pl.pallas_call(
        functools.partial(_merge_kernel, tn=tn, n_split=len(wa_specs)),
        grid=(n // tm,),
        in_specs=[
            pl.BlockSpec((tm, d), lambda i: (i, 0)),
            pl.BlockSpec((tm, z.shape[1]), lambda i: (i, 0)),
            pl.BlockSpec((tm, 2 * d), lambda i: (i, 0)),
            *wa_specs, *wr_specs,
        ],
        out_specs=pl.BlockSpec((tm, d), lambda i: (i, 0)),
        out_shape=jax.ShapeDtypeStruct((n, d), BF16),
        compiler_params=_params(1),
    )(o, z, gates, *([wa] * len(wa_specs)), *([wr] * len(wr_specs)))


def _out_ln_kernel(m_ref, x_ref, *refs, tn, alpha):
    *w_refs, g_ref, b_ref, y_ref = refs
    for r in range(m_ref.shape[0] // ROW_SUB):
        rs = slice(r * ROW_SUB, (r + 1) * ROW_SUB)
        m = m_ref[rs, :]
        chunks = _col_chunks(w_refs, tn)
        v = [alpha * x_ref[rs, gc] + jnp.dot(m, w_ref[:, lc], preferred_element_type=F32) for w_ref, lc, gc in chunks]
        inv_d = 1.0 / y_ref.shape[1]
        mu = sum(jnp.sum(vc, axis=-1, keepdims=True) for vc in v) * inv_d
        dv = [vc - mu for vc in v]
        var = sum(jnp.sum(dc * dc, axis=-1, keepdims=True) for dc in dv) * inv_d
        rstd = lax.rsqrt(var + LN_EPS)
        for dc, (_, _, gc) in zip(dv, chunks):
            y_ref[rs, gc] = dc * rstd * g_ref[:, gc] + b_ref[:, gc]


def _out_ln(m, x, w, g, b, alpha, *, tm=1024, tn=COL_UNIT):
    n, d = x.shape
    w_specs = _resident_split(w.shape)
    return pl.pallas_call(
        functools.partial(_out_ln_kernel, tn=tn, alpha=alpha),
        grid=(n // tm,),
        in_specs=[
            pl.BlockSpec((tm, m.shape[1]), lambda i: (i, 0)),
            pl.BlockSpec((tm, d), lambda i: (i, 0)),
            *w_specs, _resident((1, d)), _resident((1, d)),
        ],
        out_specs=pl.BlockSpec((tm, d), lambda i: (i, 0)),
        out_shape=jax.ShapeDtypeStruct((n, d), F32),
        compiler_params=_params(1),
    )(m, x, *([w] * len(w_specs)), g.reshape(1, d), b.reshape(1, d))


def _ffn_kernel(x_ref, wu_ref, wg_ref, cw_ref, cb_ref, wd_ref, g_ref, b_ref, y_ref,
                xb_s, gs_s, tail_s, *, tm, alpha, blocks_per_seq):
    i = pl.program_id(0)
    c = pl.program_id(1)
    first = (i % blocks_per_seq) == 0

    @pl.when(c == 0)
    def _():
        xb_s[...] = x_ref[...].astype(BF16)
        y_ref[...] = alpha * x_ref[...]

    @pl.when(first)
    def _():
        gs_s[0:HALO] = jnp.zeros((HALO, gs_s.shape[1]), F32)

    @pl.when(jnp.logical_not(first))
    def _():
        gs_s[0:HALO] = tail_s[c]

    width = cw_ref.shape[0]
    th = tm // 2
    hids = []
    for rh in range(2):
        r0 = th * rh
        xb = xb_s[r0:r0 + th]
        gs_s[HALO + r0:HALO + r0 + th] = jnp.dot(xb, wg_ref[...], preferred_element_type=F32)
        up = jnp.dot(xb, wu_ref[...], preferred_element_type=F32)
        gate = cb_ref[...]
        for k in range(width):
            off = HALO - (width - 1) + k + r0
            gate = gate + cw_ref[k:k + 1, :] * gs_s[off:off + th, :]
        hids.append((jax.nn.gelu(gate, approximate=True) * up).astype(BF16))
    tail_s[c] = gs_s[tm:tm + HALO]
    y_ref[...] += jnp.dot(jnp.concatenate(hids, axis=0), wd_ref[...], preferred_element_type=F32)

    @pl.when(c == pl.num_programs(1) - 1)
    def _():
        y_ref[...] = _layer_norm(y_ref[...], g_ref[...], b_ref[...])


def _ffn(x, wu, wg, conv_w, conv_b, wd, g, b, alpha, seq, *, tm=512, tf=1024):
    n, d = x.shape
    dff = wu.shape[1]
    return pl.pallas_call(
        functools.partial(_ffn_kernel, tm=tm, alpha=alpha, blocks_per_seq=seq // tm),
        grid=(n // tm, dff // tf),
        in_specs=[
            pl.BlockSpec((tm, d), lambda i, c: (i, 0)),
            pl.BlockSpec((d, tf), lambda i, c: (0, c)),
            pl.BlockSpec((d, tf), lambda i, c: (0, c)),
            pl.BlockSpec((conv_w.shape[0], tf), lambda i, c: (0, c)),
            pl.BlockSpec((1, tf), lambda i, c: (0, c)),
            pl.BlockSpec((tf, d), lambda i, c: (c, 0)),
            _resident((1, d)), _resident((1, d)),
        ],
        out_specs=pl.BlockSpec((tm, d), lambda i, c: (i, 0)),
        out_shape=jax.ShapeDtypeStruct((n, d), F32),
        scratch_shapes=[pltpu.VMEM((tm, d), BF16), pltpu.VMEM((HALO + tm, tf), F32),
                        pltpu.VMEM((dff // tf, HALO, tf), F32)],
        compiler_params=_params(2),
    )(x, wu, wg, conv_w, conv_b.reshape(1, dff), wd, g.reshape(1, d), b.reshape(1, d))


def kernel(x, w_in, b_gate, rnn_conv_w, rnn_conv_b, lru_wa, lru_ba, lru_wi, lru_bi, lru_lambda, attn_sinks,
           w_attn_proj, w_rnn_proj, w_out, ln1_g, ln1_b, ffn_w_up, ffn_w_gate, ffn_conv_w, ffn_conv_b,
           ffn_w_down, ln2_g, ln2_b):
    bsz, seq, d = x.shape
    depth = w_in.shape[0]
    alpha = float((2 * depth) ** 0.25)
    d_attn = N_Q_HEADS * HEAD_DIM
    d_qkv = d_attn + 2 * N_KV_HEADS * HEAD_DIM
    d_rnn = rnn_conv_w.shape[-1]

    h = x.reshape(bsz * seq, d)
    for l in range(depth):
        w_qkv, w_rxy, w_gl_half = _cast_bf16(w_in[l], (0, d_qkv, d_qkv + 2 * d_rnn, w_in.shape[-1]),
                                             (1.0, 1.0, 0.5))
        qkv, hb = _qkv_proj(h, w_qkv, d_attn, HEAD_DIM ** -0.5 * LOG2E)
        gates = _gates_proj(hb, w_gl_half, b_gate[l])
        o = _attention(qkv, attn_sinks[l], bsz, seq)

        wg = _gate_superblocks(lru_wa[l], lru_wi[l])
        z = _rnn(h, w_rxy, rnn_conv_w[l], rnn_conv_b[l], wg, lru_ba[l], lru_bi[l], lru_lambda[l], bsz, seq)

        m = _merge(o, z, gates, _cast_bf16(w_attn_proj[l]), _cast_bf16(w_rnn_proj[l]))
        h = _out_ln(m, h, _cast_bf16(w_out[l]), ln1_g[l], ln1_b[l], alpha)
        h = _ffn(h, _cast_bf16(ffn_w_up[l]), _cast_bf16(ffn_w_gate[l]), ffn_conv_w[l], ffn_conv_b[l],
                 _cast_bf16(ffn_w_down[l]), ln2_g[l], ln2_b[l], alpha, seq)
    return h.reshape(bsz, seq, d)
```

```python
import functools

import numpy as np
import jax
import jax.numpy as jnp
from jax import lax
from jax.experimental import pallas as pl
from jax.experimental.pallas import tpu as pltpu

F32 = jnp.float32
BF16 = jnp.bfloat16

HEAD_DIM = 64
N_Q_HEADS = 32
N_KV_HEADS = 4
GROUP = N_Q_HEADS // N_KV_HEADS
WINDOW = 128
N_RNN_BLOCKS = 16
RNN_SUPER = 4
LRU_C = 8.0
LN_EPS = 1e-5
LOG2E = 1.4426950408889634
GELU_C0 = 0.7978845608028654
GELU_C1 = GELU_C0 * 0.044715
HALO = 16
SUBLANES = 8
LANES = 128
COL_UNIT = 512
ROW_SUB = 512
VMEM_LIMIT = 56 * 1024 * 1024


def _params(n_axes, vmem_limit=VMEM_LIMIT):
    return pltpu.CompilerParams(dimension_semantics=("arbitrary",) * n_axes,
                                vmem_limit_bytes=vmem_limit)


def _resident(shape):
    nd = len(shape)
    return pl.BlockSpec(shape, lambda *_: (0,) * nd, pipeline_mode=pl.Buffered(1))


def _split_cols(width):
    if width % (2 * COL_UNIT):
        return [(width, 0)]
    return [(width - COL_UNIT, 0), (COL_UNIT, width - COL_UNIT)]


def _resident_split(shape):
    rows, width = shape
    return [pl.BlockSpec((rows, w), functools.partial(lambda j, *_: (0, j), off // w), pipeline_mode=pl.Buffered(1))
            for w, off in _split_cols(width)]


def _col_chunks(refs, tn):
    out, off = [], 0
    for r in refs:
        for c in range(r.shape[1] // tn):
            out.append((r, slice(c * tn, (c + 1) * tn), slice(off, off + tn)))
            off += tn
    return out


def _gelu_tanh(x):
    hx = 0.5 * x
    return hx * jnp.tanh(x * (GELU_C0 + GELU_C1 * (x * x))) + hx


def _layer_norm(v, g, b):
    mu = jnp.mean(v, axis=-1, keepdims=True)
    d = v - mu
    var = jnp.mean(d * d, axis=-1, keepdims=True)
    return d * lax.rsqrt(var + LN_EPS) * g + b


def _cast_kernel(w_ref, o_ref):
    o_ref[...] = w_ref[...].astype(o_ref.dtype)


def _cast_bf16(w, cols, *, block_bytes=8 * 1024 * 1024):
    rows = w.shape[0]
    lo, hi = cols
    width = hi - lo
    assert lo % width == 0
    tr = rows
    while tr * width * 4 > block_bytes and tr % 32 == 0:
        tr //= 2
    return pl.pallas_call(
        _cast_kernel,
        grid=(rows // tr,),
        in_specs=[pl.BlockSpec((tr, width), lambda i: (i, lo // width))],
        out_specs=pl.BlockSpec((tr, width), lambda i: (i, 0)),
        out_shape=jax.ShapeDtypeStruct((rows, width), BF16),
        compiler_params=_params(1),
    )(w)


def _side_specs(jobs, steps, row_index):
    in_specs, out_specs, out_shapes = [], [], []
    for w, splits in jobs:
        rows, cols = w.shape
        tr = rows // steps
        assert tr * steps == rows and tr % 16 == 0
        in_specs.append(pl.BlockSpec((tr, cols), lambda *g: (row_index(*g), 0)))
        for lo, hi, _ in splits:
            out_specs.append(pl.BlockSpec((tr, hi - lo), lambda *g: (row_index(*g), 0)))
            out_shapes.append(jax.ShapeDtypeStruct((rows, hi - lo), BF16))
    return in_specs, out_specs, out_shapes


def _side_splits(jobs):
    return tuple(splits for _, splits in jobs)


def _run_side_casts(side, in_refs, out_refs):
    out_refs = list(out_refs)
    for splits, w_ref in zip(side, in_refs):
        for lo, hi, scale in splits:
            w = w_ref[:, lo:hi]
            out_refs.pop(0)[...] = (w if scale == 1.0 else w * scale).astype(BF16)


def _n_side_outs(side):
    return sum(len(splits) for splits in side)


def _whole(w):
    return (w, ((0, w.shape[1], 1.0),))


def _qkv_kernel(x_ref, w_ref, *refs, tn, n_scaled, scale, side):
    side_in, (o_ref, xb_ref), side_out = refs[:len(side)], refs[len(side):len(side) + 2], refs[len(side) + 2:]
    _run_side_casts(side, side_in, side_out)
    for r in range(x_ref.shape[0] // ROW_SUB):
        rs = slice(r * ROW_SUB, (r + 1) * ROW_SUB)
        xb = x_ref[rs, :].astype(BF16)
        xb_ref[rs, :] = xb
        for c in range(w_ref.shape[1] // tn):
            cs = slice(c * tn, (c + 1) * tn)
            acc = jnp.dot(xb, w_ref[:, cs], preferred_element_type=F32)
            if c < n_scaled:
                acc = acc * scale
            o_ref[rs, cs] = acc.astype(o_ref.dtype)


def _qkv_proj(x, w, d_attn, scale, side_jobs=(), *, tm=512, tn=512):
    n, d = x.shape
    nc = w.shape[1]
    s_in, s_out, s_shapes = _side_specs(side_jobs, n // tm, lambda i: i)
    return pl.pallas_call(
        functools.partial(_qkv_kernel, tn=tn, n_scaled=d_attn // tn, scale=scale, side=_side_splits(side_jobs)),
        grid=(n // tm,),
        in_specs=[pl.BlockSpec((tm, d), lambda i: (i, 0)), _resident((d, nc)), *s_in],
        out_specs=[pl.BlockSpec((tm, nc), lambda i: (i, 0)), pl.BlockSpec((tm, d), lambda i: (i, 0)), *s_out],
        out_shape=[jax.ShapeDtypeStruct((n, nc), BF16), jax.ShapeDtypeStruct((n, d), BF16), *s_shapes],
        compiler_params=_params(1),
    )(x, w, *[w_ for w_, _ in side_jobs])


def _gates_kernel(x_ref, *refs, tn, n_w, side):
    w_refs, b_ref = refs[:n_w], refs[n_w]
    side_in, o_ref, side_out = refs[n_w + 1:n_w + 1 + len(side)], refs[n_w + 1 + len(side)], refs[n_w + 2 + len(side):]
    _run_side_casts(side, side_in, side_out)
    for r in range(x_ref.shape[0] // ROW_SUB):
        rs = slice(r * ROW_SUB, (r + 1) * ROW_SUB)
        xb = x_ref[rs, :]
        for w_ref, lc, gc in _col_chunks(w_refs, tn):
            half_z = jnp.dot(xb, w_ref[:, lc], preferred_element_type=F32) + 0.5 * b_ref[:, gc]
            o_ref[rs, gc] = (0.5 * jnp.tanh(half_z) + 0.5).astype(o_ref.dtype)


def _gates_proj(xb, w, bias, side_jobs=(), *, tm=1024, tn=COL_UNIT):
    n, d = xb.shape
    nc = w.shape[1]
    w_specs = _resident_split((d, nc))
    s_in, s_out, s_shapes = _side_specs(side_jobs, n // tm, lambda i: i)
    return pl.pallas_call(
        functools.partial(_gates_kernel, tn=tn, n_w=len(w_specs), side=_side_splits(side_jobs)),
        grid=(n // tm,),
        in_specs=[pl.BlockSpec((tm, d), lambda i: (i, 0)), *w_specs, _resident((1, nc)), *s_in],
        out_specs=[pl.BlockSpec((tm, nc), lambda i: (i, 0)), *s_out],
        out_shape=[jax.ShapeDtypeStruct((n, nc), BF16), *s_shapes],
        compiler_params=_params(1),
    )(xb, *([w] * len(w_specs)), bias.reshape(1, nc), *[w_ for w_, _ in side_jobs])


def _alibi_slopes():
    h = np.arange(1, N_Q_HEADS + 1, dtype=np.float32)
    return [float(s) * LOG2E for s in (2.0 ** (-8.0 * h / N_Q_HEADS)).astype(np.float32)]


def _split_halves(xp, odd, lo):
    xr = pltpu.roll(xp, HEAD_DIM, axis=1)
    zero = jnp.zeros_like(xp)
    if odd:
        return jnp.where(lo, xr, zero).astype(BF16), jnp.where(lo, zero, xp).astype(BF16)
    return jnp.where(lo, xp, zero).astype(BF16), jnp.where(lo, zero, xr).astype(BF16)


def _attn_kernel(sinks_ref, q_ref, kvp_ref, kvc_ref, *refs, tq, slopes, side):
    side_in, o_ref = refs[:len(side)], refs[len(side)]
    side_out, p_s = refs[len(side) + 1:-1], refs[-1]
    _run_side_casts(side, side_in, side_out)
    t = pl.program_id(1)
    half = GROUP // 2
    qi = lax.broadcasted_iota(jnp.int32, (WINDOW, 2 * WINDOW), 0)
    kj = lax.broadcasted_iota(jnp.int32, (WINDOW, 2 * WINDOW), 1)
    dist = WINDOW + qi - kj
    valid = (dist >= 0) & (dist < WINDOW)
    before_start = (kj < WINDOW) & (t == 0)
    distf = dist.astype(F32)
    lo = lax.broadcasted_iota(jnp.int32, (1, 2 * HEAD_DIM), 1) < HEAD_DIM
    ones_lo = jnp.where(lo, 1.0, 0.0).astype(BF16)
    ones_hi = jnp.where(lo, 0.0, 1.0).astype(BF16)
    kv = jnp.concatenate([kvp_ref[...], kvc_ref[...]], axis=0)
    nkv = kv.shape[0]
    dk = N_KV_HEADS * HEAD_DIM
    for h in range(N_KV_HEADS):
        pair, odd = divmod(h, 2)
        k_lo, k_hi = _split_halves(kv[:, 128 * pair:128 * pair + 128].astype(F32), odd, lo)
        v_lo, v_hi = _split_halves(kv[:, dk + 128 * pair:dk + 128 * pair + 128].astype(F32), odd, lo)
        v_lo = jnp.concatenate([v_lo, jnp.broadcast_to(ones_lo, (nkv, 2 * HEAD_DIM))], axis=1)
        v_hi = jnp.concatenate([v_hi, jnp.broadcast_to(ones_hi, (nkv, 2 * HEAD_DIM))], axis=1)
        bias = [[jnp.where(valid, slopes[GROUP * h + 2 * pp + par] * distf, jnp.inf) for par in range(2)]
                for pp in range(half)]
        for qb in range(tq // WINDOW):
            rs = slice(WINDOW * qb, WINDOW * qb + 2 * WINDOW)
            kk = jnp.concatenate([k_lo[rs], k_hi[rs]], axis=0)
            vv = jnp.concatenate([v_lo[rs], v_hi[rs]], axis=0)
            ql = jnp.concatenate(
                [q_ref[WINDOW * qb:WINDOW * (qb + 1), 512 * h + 128 * pp:512 * h + 128 * (pp + 1)]
                 for pp in range(half)], axis=0)
            s = lax.dot_general(ql, kk, (((1,), (1,)), ((), ())), preferred_element_type=F32)
            sink_terms = []
            for pp in range(half):
                terms = []
                for par in range(2):
                    sink = sinks_ref[GROUP * h + 2 * pp + par] * LOG2E
                    sp = s[WINDOW * pp:WINDOW * (pp + 1), 2 * WINDOW * par:2 * WINDOW * (par + 1)] - bias[pp][par]
                    if qb == 0:
                        sp = jnp.where(before_start, -jnp.inf, sp)
                    m = jnp.maximum(jnp.max(sp, axis=-1, keepdims=True), sink)
                    p_s[WINDOW * pp:WINDOW * (pp + 1), 2 * WINDOW * par:2 * WINDOW * (par + 1)] = (
                        jnp.exp2(sp - m).astype(BF16))
                    terms.append(jnp.exp2(sink - m))
                sink_terms.append(jnp.where(lo, terms[0], terms[1]))
            pv = jnp.dot(p_s[...], vv, preferred_element_type=F32)
            for pp in range(half):
                rows = slice(WINDOW * pp, WINDOW * (pp + 1))
                den = pv[rows, 2 * HEAD_DIM:] + sink_terms[pp]
                o_ref[WINDOW * qb:WINDOW * (qb + 1), 512 * h + 128 * pp:512 * h + 128 * (pp + 1)] = (
                    pv[rows, :2 * HEAD_DIM] / den).astype(o_ref.dtype)


def _attention(qkv, sinks, bsz, seq, side_jobs=(), *, tq=512):
    n = qkv.shape[0]
    d_attn = N_Q_HEADS * HEAD_DIM
    d_kv2 = 2 * N_KV_HEADS * HEAD_DIM
    nt = seq // tq
    per_w = tq // WINDOW
    kv_col = d_attn // d_kv2
    s_in, s_out, s_shapes = _side_specs(side_jobs, bsz * nt, lambda b, t: b * nt + t)
    return pl.pallas_call(
        functools.partial(_attn_kernel, tq=tq, slopes=_alibi_slopes(), side=_side_splits(side_jobs)),
        grid=(bsz, nt),
        in_specs=[
            pl.BlockSpec(memory_space=pltpu.SMEM),
            pl.BlockSpec((tq, d_attn), lambda b, t: (b * nt + t, 0)),
            pl.BlockSpec((WINDOW, d_kv2),
                         lambda b, t: (b * nt * per_w + jnp.maximum(t * per_w - 1, 0), kv_col)),
            pl.BlockSpec((tq, d_kv2), lambda b, t: (b * nt + t, kv_col)),
            *s_in,
        ],
        out_specs=[pl.BlockSpec((tq, d_attn), lambda b, t: (b * nt + t, 0)), *s_out],
        out_shape=[jax.ShapeDtypeStruct((n, d_attn), BF16), *s_shapes],
        scratch_shapes=[pltpu.VMEM((GROUP // 2 * WINDOW, 4 * WINDOW), BF16)],
        compiler_params=_params(2),
    )(sinks, qkv, qkv, qkv, *[w_ for w_, _ in side_jobs])


def _rnn_kernel(x_ref, wrx_ref, wry_ref, cw_ref, cb_ref, wg_ref, ba_ref, bi_ref, lam_ref, z_ref,
                rx_s, gy_s, xc_s, pa_s, pu_s, hp_s, tail_s, carry_s, *, tt, tn):
    t = pl.program_id(1)
    c_all = z_ref.shape[1]
    csb = c_all // RNN_SUPER
    n_slabs = c_all // LANES
    per_chunk = tn // LANES
    ng = tt // SUBLANES
    width = cw_ref.shape[0]

    @pl.when(t == 0)
    def _():
        carry_s[...] = jnp.zeros_like(carry_s)
        tail_s[...] = jnp.zeros_like(tail_s)

    xb = x_ref[...].astype(BF16)
    for c in range(c_all // tn):
        acc = jnp.dot(xb, wrx_ref[:, c * tn:(c + 1) * tn], preferred_element_type=F32)
        for k in range(per_chunk):
            rx_s[c * per_chunk + k] = acc[:, LANES * k:LANES * (k + 1)]

    def ry_chunk(c):
        ry = jnp.dot(xb, wry_ref[:, c * tn:(c + 1) * tn], preferred_element_type=F32)
        gy = _gelu_tanh(ry)
        for k in range(per_chunk):
            gy_s[c * per_chunk + k] = gy[:, LANES * k:LANES * (k + 1)]

    ry_chunks = list(range(c_all // tn))

    first_row = lax.broadcasted_iota(jnp.int32, (ng, LANES), 0) == 0
    for s in range(n_slabs):
        ls = slice(LANES * s, LANES * (s + 1))
        x_ph = [rx_s[s, pl.ds(j, ng, stride=SUBLANES), :] for j in range(SUBLANES)]
        x_prev = {m: jnp.where(first_row, tail_s[s, m:m + 1, :], pltpu.roll(x_ph[m], 1, axis=0))
                  for m in range(SUBLANES - (width - 1), SUBLANES)}
        for j in range(SUBLANES):
            acc = cb_ref[:, ls]
            for k in range(width):
                src = j - (width - 1) + k
                acc = acc + cw_ref[k:k + 1, ls] * (x_ph[src] if src >= 0 else x_prev[src + SUBLANES])
            xc_s[ng * j:ng * (j + 1), ls] = acc
        tail_s[s] = rx_s[s, tt - SUBLANES:tt, :]

    for sb in range(RNN_SUPER):
        cs = slice(csb * sb, csb * (sb + 1))
        xc = xc_s[:, cs]
        pre = jnp.dot(xc.astype(BF16), wg_ref[sb], preferred_element_type=F32)
        n_now = -(-len(ry_chunks) // (RNN_SUPER - sb))
        for c in ry_chunks[:n_now]:
            ry_chunk(c)
        ry_chunks = ry_chunks[n_now:]
        ta = jnp.tanh(pre[:, :csb] + 0.5 * ba_ref[:, cs])
        ti = jnp.tanh(pre[:, csb:] + 0.5 * bi_ref[:, cs])
        half_c_sp = (0.5 * LRU_C) * jax.nn.softplus(-lam_ref[:, cs])
        neg_log_a = ta * half_c_sp + half_c_sp
        a = jnp.exp2(neg_log_a * (-LOG2E))
        hxc = 0.5 * xc
        ix = hxc * ti + hxc
        u = jnp.sqrt(jnp.tanh(neg_log_a) * (a * a + 1.0)) * ix
        pa = a[0:ng]
        pu = u[0:ng]
        pa_s[0:ng, cs] = pa
        pu_s[0:ng, cs] = pu
        for j in range(1, SUBLANES):
            rows = slice(ng * j, ng * (j + 1))
            pu = a[rows] * pu + u[rows]
            pa = a[rows] * pa
            pa_s[rows, cs] = pa
            pu_s[rows, cs] = pu

    last = ng * (SUBLANES - 1)

    def body(g, carry):
        hp_s[pl.ds(g, 1), :] = carry
        return pa_s[pl.ds(last + g, 1), :] * carry + pu_s[pl.ds(last + g, 1), :]

    carry_s[...] = lax.fori_loop(0, ng, body, carry_s[...], unroll=4)

    for s in range(n_slabs):
        ls = slice(LANES * s, LANES * (s + 1))
        hp = hp_s[:, ls]
        for j in range(SUBLANES):
            rows = slice(ng * j, ng * (j + 1))
            h = pa_s[rows, ls] * hp + pu_s[rows, ls]
            rx_s[s, pl.ds(j, ng, stride=SUBLANES), :] = h * gy_s[s, pl.ds(j, ng, stride=SUBLANES), :]
        z_ref[:, ls] = rx_s[s].astype(z_ref.dtype)


def _rnn(x, w_rxy, conv_w, conv_b, wg, ba, bi, lam, bsz, seq, *, tt=256, tn=512):
    n, d = x.shape
    c = w_rxy.shape[1] // 2
    nt = seq // tt
    row = lambda b, t: (b * nt + t, 0)
    vec = lambda a: a.reshape(1, c)
    slab = lambda rows: pltpu.VMEM((c // LANES, rows, LANES), F32)
    return pl.pallas_call(
        functools.partial(_rnn_kernel, tt=tt, tn=tn),
        grid=(bsz, nt),
        in_specs=[
            pl.BlockSpec((tt, d), row),
            pl.BlockSpec((d, c), lambda b, t: (0, 0), pipeline_mode=pl.Buffered(1)),
            pl.BlockSpec((d, c), lambda b, t: (0, 1), pipeline_mode=pl.Buffered(1)),
            _resident(conv_w.shape), _resident((1, c)), _resident(wg.shape),
            _resident((1, c)), _resident((1, c)), _resident((1, c)),
        ],
        out_specs=pl.BlockSpec((tt, c), row),
        out_shape=jax.ShapeDtypeStruct((n, c), BF16),
        scratch_shapes=[
            slab(tt), slab(tt),
            pltpu.VMEM((tt, c), F32), pltpu.VMEM((tt, c), F32), pltpu.VMEM((tt, c), F32),
            pltpu.VMEM((tt // SUBLANES, c), F32),
            slab(SUBLANES),
            pltpu.VMEM((1, c), F32),
        ],
        compiler_params=_params(2),
    )(x, w_rxy, w_rxy, conv_w, vec(conv_b), wg, vec(ba), vec(bi), vec(lam))


def _superblock_kernel(wa_ref, wi_ref, o_ref):
    per, bw, _ = wa_ref.shape
    wide = per * bw
    row = lax.broadcasted_iota(jnp.int32, (bw, wide), 0)
    col = lax.broadcasted_iota(jnp.int32, (bw, wide), 1)
    for i in range(per):
        place = jnp.where(col == row + bw * i, 0.5, 0.0).astype(BF16)
        rs = slice(bw * i, bw * (i + 1))
        o_ref[0, rs, 0:wide] = jnp.dot(wa_ref[i].astype(BF16), place, preferred_element_type=F32).astype(o_ref.dtype)
        o_ref[0, rs, wide:2 * wide] = jnp.dot(wi_ref[i].astype(BF16), place,
                                              preferred_element_type=F32).astype(o_ref.dtype)


def _gate_superblocks(wa, wi):
    nb, bw, _ = wa.shape
    per = nb // RNN_SUPER
    blk = pl.BlockSpec((per, bw, bw), lambda s: (s, 0, 0))
    return pl.pallas_call(
        _superblock_kernel,
        grid=(RNN_SUPER,),
        in_specs=[blk, blk],
        out_specs=pl.BlockSpec((1, per * bw, 2 * per * bw), lambda s: (s, 0, 0)),
        out_shape=jax.ShapeDtypeStruct((RNN_SUPER, per * bw, 2 * per * bw), BF16),
        compiler_params=_params(1),
    )(wa, wi)


def _merge_kernel(o_ref, z_ref, g_ref, *refs, tn, n_split, side):
    wa_refs, wr_refs, refs = refs[:n_split], refs[n_split:2 * n_split], refs[2 * n_split:]
    side_in, m_ref, side_out = refs[:len(side)], refs[len(side)], refs[len(side) + 1:]
    _run_side_casts(side, side_in, side_out)
    o = o_ref[...]
    z = z_ref[...]
    d = m_ref.shape[1]
    for (wa_ref, lc, gc), (wr_ref, _, _) in zip(_col_chunks(wa_refs, tn), _col_chunks(wr_refs, tn)):
        ya = jnp.dot(o, wa_ref[:, lc], preferred_element_type=F32)
        yr = jnp.dot(z, wr_ref[:, lc], preferred_element_type=F32)
        ga = g_ref[:, gc].astype(F32)
        gr = g_ref[:, d + gc.start:d + gc.stop].astype(F32)
        m_ref[:, gc] = (ga * ya + gr * yr).astype(m_ref.dtype)


def _merge(o, z, gates, wa, wr, side_jobs=(), *, tm=512, tn=COL_UNIT):
    n, d = o.shape
    wa_specs, wr_specs = _resident_split(wa.shape), _resident_split(wr.shape)
    assert len(wa_specs) == len(wr_specs)
    s_in, s_out, s_shapes = _side_specs(side_jobs, n // tm, lambda i: i)
    return pl.pallas_call(
        functools.partial(_merge_kernel, tn=tn, n_split=len(wa_specs), side=_side_splits(side_jobs)),
        grid=(n // tm,),
        in_specs=[
            pl.BlockSpec((tm, d), lambda i: (i, 0)),
            pl.BlockSpec((tm, z.shape[1]), lambda i: (i, 0)),
            pl.BlockSpec((tm, 2 * d), lambda i: (i, 0)),
            *wa_specs, *wr_specs, *s_in,
        ],
        out_specs=[pl.BlockSpec((tm, d), lambda i: (i, 0)), *s_out],
        out_shape=[jax.ShapeDtypeStruct((n, d), BF16), *s_shapes],
        compiler_params=_params(1),
    )(o, z, gates, *([wa] * len(wa_specs)), *([wr] * len(wr_specs)), *[w_ for w_, _ in side_jobs])


def _out_ln_kernel(m_ref, x_ref, *refs, tn, alpha):
    *w_refs, g_ref, b_ref, y_ref = refs
    for r in range(m_ref.shape[0] // ROW_SUB):
        rs = slice(r * ROW_SUB, (r + 1) * ROW_SUB)
        m = m_ref[rs, :]
        for w_ref, lc, gc in _col_chunks(w_refs, tn):
            y_ref[rs, gc] = alpha * x_ref[rs, gc] + jnp.dot(m, w_ref[:, lc], preferred_element_type=F32)
        y_ref[rs, :] = _layer_norm(y_ref[rs, :], g_ref[...], b_ref[...])


def _out_ln(m, x, w, g, b, alpha, *, tm=1024, tn=COL_UNIT):
    n, d = x.shape
    w_specs = _resident_split(w.shape)
    return pl.pallas_call(
        functools.partial(_out_ln_kernel, tn=tn, alpha=alpha),
        grid=(n // tm,),
        in_specs=[
            pl.BlockSpec((tm, m.shape[1]), lambda i: (i, 0)),
            pl.BlockSpec((tm, d), lambda i: (i, 0)),
            *w_specs, _resident((1, d)), _resident((1, d)),
        ],
        out_specs=pl.BlockSpec((tm, d), lambda i: (i, 0)),
        out_shape=jax.ShapeDtypeStruct((n, d), F32),
        compiler_params=_params(1),
    )(m, x, *([w] * len(w_specs)), g.reshape(1, d), b.reshape(1, d))


def _ffn_kernel(x_ref, wu_ref, wg_ref, cw_ref, cb_ref, wd_ref, g_ref, b_ref, y_ref,
                xb_s, gs_s, tail_s, *, tm, alpha, blocks_per_seq):
    i = pl.program_id(0)
    c = pl.program_id(1)
    first = (i % blocks_per_seq) == 0

    @pl.when(c == 0)
    def _():
        xb_s[...] = x_ref[...].astype(BF16)
        y_ref[...] = alpha * x_ref[...]

    @pl.when(first)
    def _():
        gs_s[0:HALO] = jnp.zeros((HALO, gs_s.shape[1]), F32)

    @pl.when(jnp.logical_not(first))
    def _():
        gs_s[0:HALO] = tail_s[c]

    xb = xb_s[...]
    gs_s[HALO:HALO + tm] = jnp.dot(xb, wg_ref[...], preferred_element_type=F32)
    tail_s[c] = gs_s[tm:tm + HALO]
    up = jnp.dot(xb, wu_ref[...], preferred_element_type=F32)
    width = cw_ref.shape[0]
    gate = cb_ref[...]
    for k in range(width):
        off = HALO - (width - 1) + k
        gate = gate + cw_ref[k:k + 1, :] * gs_s[off:off + tm, :]
    hid = (jax.nn.gelu(gate, approximate=True) * up).astype(BF16)
    y_ref[...] += jnp.dot(hid, wd_ref[...], preferred_element_type=F32)

    @pl.when(c == pl.num_programs(1) - 1)
    def _():
        y_ref[...] = _layer_norm(y_ref[...], g_ref[...], b_ref[...])


def _ffn(x, wu, wg, conv_w, conv_b, wd, g, b, alpha, seq, *, tm=512, tf=1024):
    n, d = x.shape
    dff = wu.shape[1]
    return pl.pallas_call(
        functools.partial(_ffn_kernel, tm=tm, alpha=alpha, blocks_per_seq=seq // tm),
        grid=(n // tm, dff // tf),
        in_specs=[
            pl.BlockSpec((tm, d), lambda i, c: (i, 0)),
            pl.BlockSpec((d, tf), lambda i, c: (0, c)),
            pl.BlockSpec((d, tf), lambda i, c: (0, c)),
            pl.BlockSpec((conv_w.shape[0], tf), lambda i, c: (0, c)),
            pl.BlockSpec((1, tf), lambda i, c: (0, c)),
            pl.BlockSpec((tf, d), lambda i, c: (c, 0)),
            _resident((1, d)), _resident((1, d)),
        ],
        out_specs=pl.BlockSpec((tm, d), lambda i, c: (i, 0)),
        out_shape=jax.ShapeDtypeStruct((n, d), F32),
        scratch_shapes=[pltpu.VMEM((tm, d), BF16), pltpu.VMEM((HALO + tm, tf), F32),
                        pltpu.VMEM((dff // tf, HALO, tf), F32)],
        compiler_params=_params(2),
    )(x, wu, wg, conv_w, conv_b.reshape(1, dff), wd, g.reshape(1, d), b.reshape(1, d))


def kernel(x, w_in, b_gate, rnn_conv_w, rnn_conv_b, lru_wa, lru_ba, lru_wi, lru_bi, lru_lambda, attn_sinks,
           w_attn_proj, w_rnn_proj, w_out, ln1_g, ln1_b, ffn_w_up, ffn_w_gate, ffn_conv_w, ffn_conv_b,
           ffn_w_down, ln2_g, ln2_b):
    bsz, seq, d = x.shape
    depth = w_in.shape[0]
    alpha = float((2 * depth) ** 0.25)
    d_attn = N_Q_HEADS * HEAD_DIM
    d_qkv = d_attn + 2 * N_KV_HEADS * HEAD_DIM
    d_rnn = rnn_conv_w.shape[-1]

    h = x.reshape(bsz * seq, d)
    for l in range(depth):
        d_in = w_in.shape[-1]
        w_qkv = _cast_bf16(w_in[l], cols=(0, d_qkv))
        rest_of_w_in = (w_in[l], ((d_qkv, d_qkv + 2 * d_rnn, 1.0), (d_qkv + 2 * d_rnn, d_in, 0.5)))
        qkv, hb, w_rxy, w_gl_half = _qkv_proj(h, w_qkv, d_attn, HEAD_DIM ** -0.5 * LOG2E, (rest_of_w_in,))
        gates, w_up = _gates_proj(hb, w_gl_half, b_gate[l], (_whole(ffn_w_up[l]),))
        o, w_gate, w_ap, w_rp = _attention(qkv, attn_sinks[l], bsz, seq,
                                           (_whole(ffn_w_gate[l]), _whole(w_attn_proj[l]), _whole(w_rnn_proj[l])))

        wg = _gate_superblocks(lru_wa[l], lru_wi[l])
        z = _rnn(h, w_rxy, rnn_conv_w[l], rnn_conv_b[l], wg, lru_ba[l], lru_bi[l], lru_lambda[l], bsz, seq)

        m, w_down, w_o = _merge(o, z, gates, w_ap, w_rp, (_whole(ffn_w_down[l]), _whole(w_out[l])))
        h = _out_ln(m, h, w_o, ln1_g[l], ln1_b[l], alpha)
        h = _ffn(h, w_up, w_gate, ffn_conv_w[l], ffn_conv_b[l], w_down, ln2_g[l], ln2_b[l], alpha, seq)
    return h.reshape(bsz, seq, d)
```

```python
import functools

import numpy as np
import jax
import jax.numpy as jnp
from jax import lax
from jax.experimental import pallas as pl
from jax.experimental.pallas import tpu as pltpu

F32 = jnp.float32
BF16 = jnp.bfloat16

HEAD_DIM = 64
N_Q_HEADS = 32
N_KV_HEADS = 4
GROUP = N_Q_HEADS // N_KV_HEADS
WINDOW = 128
N_RNN_BLOCKS = 16
RNN_SUPER = 4
LRU_C = 8.0
LN_EPS = 1e-5
LOG2E = 1.4426950408889634
GELU_C0 = 0.7978845608028654
GELU_C1 = GELU_C0 * 0.044715
HALO = 16
SUBLANES = 8
LANES = 128
COL_UNIT = 512
MXU_TILE = 256
ROW_SUB = 512
VMEM_LIMIT = 56 * 1024 * 1024


def _params(n_axes, vmem_limit=VMEM_LIMIT):
    return pltpu.CompilerParams(dimension_semantics=("arbitrary",) * n_axes,
                                vmem_limit_bytes=vmem_limit)


def _resident(shape):
    nd = len(shape)
    return pl.BlockSpec(shape, lambda *_: (0,) * nd, pipeline_mode=pl.Buffered(1))


def _split_cols(width):
    if width % (2 * COL_UNIT):
        return [(width, 0)]
    return [(width - COL_UNIT, 0), (COL_UNIT, width - COL_UNIT)]


def _resident_split(shape):
    rows, width = shape
    return [pl.BlockSpec((rows, w), functools.partial(lambda j, *_: (0, j), off // w), pipeline_mode=pl.Buffered(1))
            for w, off in _split_cols(width)]


def _col_chunks(refs, tn):
    out, off = [], 0
    for r in refs:
        for c in range(r.shape[1] // tn):
            out.append((r, slice(c * tn, (c + 1) * tn), slice(off, off + tn)))
            off += tn
    return out


def _gelu_tanh(x):
    hx = 0.5 * x
    return hx * jnp.tanh(x * (GELU_C0 + GELU_C1 * (x * x))) + hx


def _layer_norm(v, g, b):
    mu = jnp.mean(v, axis=-1, keepdims=True)
    d = v - mu
    var = jnp.mean(d * d, axis=-1, keepdims=True)
    return d * lax.rsqrt(var + LN_EPS) * g + b


def _side_specs(jobs, steps, row_index):
    in_specs, out_specs, out_shapes = [], [], []
    for w, splits in jobs:
        rows, cols = w.shape
        tr = rows // steps
        assert tr * steps == rows and tr % 16 == 0
        in_specs.append(pl.BlockSpec((tr, cols), lambda *g: (row_index(*g), 0)))
        for lo, hi, _ in splits:
            out_specs.append(pl.BlockSpec((tr, hi - lo), lambda *g: (row_index(*g), 0)))
            out_shapes.append(jax.ShapeDtypeStruct((rows, hi - lo), BF16))
    return in_specs, out_specs, out_shapes


def _side_splits(jobs):
    return tuple(splits for _, splits in jobs)


def _run_side_casts(side, in_refs, out_refs):
    out_refs = list(out_refs)
    for splits, w_ref in zip(side, in_refs):
        for lo, hi, scale in splits:
            w = w_ref[:, lo:hi]
            out_refs.pop(0)[...] = (w if scale == 1.0 else w * scale).astype(BF16)


def _n_side_outs(side):
    return sum(len(splits) for splits in side)


def _whole(w):
    return (w, ((0, w.shape[1], 1.0),))


def _qkv_kernel(x_ref, w_ref, *refs, tn, n_scaled, scale, side):
    side_in, (o_ref, xb_ref), side_out = refs[:len(side)], refs[len(side):len(side) + 2], refs[len(side) + 2:]
    _run_side_casts(side, side_in, side_out)
    for r in range(x_ref.shape[0] // ROW_SUB):
        rs = slice(r * ROW_SUB, (r + 1) * ROW_SUB)
        xb = x_ref[rs, :].astype(BF16)
        xb_ref[rs, :] = xb
        for c in range(w_ref.shape[1] // tn):
            cs = slice(c * tn, (c + 1) * tn)
            acc = jnp.dot(xb, w_ref[:, cs].astype(BF16), preferred_element_type=F32)
            if c < n_scaled:
                acc = acc * scale
            o_ref[rs, cs] = acc.astype(o_ref.dtype)


def _qkv_proj(x, w, nc, d_attn, scale, side_jobs=(), *, tm=512, tn=512):
    n, d = x.shape
    s_in, s_out, s_shapes = _side_specs(side_jobs, n // tm, lambda i: i)
    return pl.pallas_call(
        functools.partial(_qkv_kernel, tn=tn, n_scaled=d_attn // tn, scale=scale, side=_side_splits(side_jobs)),
        grid=(n // tm,),
        in_specs=[pl.BlockSpec((tm, d), lambda i: (i, 0)), _resident((d, nc)), *s_in],
        out_specs=[pl.BlockSpec((tm, nc), lambda i: (i, 0)), pl.BlockSpec((tm, d), lambda i: (i, 0)), *s_out],
        out_shape=[jax.ShapeDtypeStruct((n, nc), BF16), jax.ShapeDtypeStruct((n, d), BF16), *s_shapes],
        compiler_params=_params(1),
    )(x, w, *[w_ for w_, _ in side_jobs])


def _gates_kernel(x_ref, *refs, tn, n_w, side):
    w_refs, b_ref = refs[:n_w], refs[n_w]
    side_in, o_ref, side_out = refs[n_w + 1:n_w + 1 + len(side)], refs[n_w + 1 + len(side)], refs[n_w + 2 + len(side):]
    _run_side_casts(side, side_in, side_out)
    for r in range(x_ref.shape[0] // ROW_SUB):
        rs = slice(r * ROW_SUB, (r + 1) * ROW_SUB)
        xb = x_ref[rs, :]
        for w_ref, lc, gc in _col_chunks(w_refs, tn):
            half_z = jnp.dot(xb, w_ref[:, lc], preferred_element_type=F32) + 0.5 * b_ref[:, gc]
            o_ref[rs, gc] = (0.5 * jnp.tanh(half_z) + 0.5).astype(o_ref.dtype)


def _gates_proj(xb, w, bias, side_jobs=(), *, tm=1024, tn=COL_UNIT):
    n, d = xb.shape
    nc = w.shape[1]
    w_specs = _resident_split((d, nc))
    s_in, s_out, s_shapes = _side_specs(side_jobs, n // tm, lambda i: i)
    return pl.pallas_call(
        functools.partial(_gates_kernel, tn=tn, n_w=len(w_specs), side=_side_splits(side_jobs)),
        grid=(n // tm,),
        in_specs=[pl.BlockSpec((tm, d), lambda i: (i, 0)), *w_specs, _resident((1, nc)), *s_in],
        out_specs=[pl.BlockSpec((tm, nc), lambda i: (i, 0)), *s_out],
        out_shape=[jax.ShapeDtypeStruct((n, nc), BF16), *s_shapes],
        compiler_params=_params(1),
    )(xb, *([w] * len(w_specs)), bias.reshape(1, nc), *[w_ for w_, _ in side_jobs])


def _alibi_slopes():
    h = np.arange(1, N_Q_HEADS + 1, dtype=np.float32)
    return [float(s) * LOG2E for s in (2.0 ** (-8.0 * h / N_Q_HEADS)).astype(np.float32)]


def _split_halves(xp, odd, lo):
    xr = pltpu.roll(xp, HEAD_DIM, axis=1)
    zero = jnp.zeros_like(xp)
    if odd:
        return jnp.where(lo, xr, zero).astype(BF16), jnp.where(lo, zero, xp).astype(BF16)
    return jnp.where(lo, xp, zero).astype(BF16), jnp.where(lo, zero, xr).astype(BF16)


def _attn_kernel(sinks_ref, q_ref, kvp_ref, kvc_ref, *refs, tq, slopes, side):
    side_in, o_ref = refs[:len(side)], refs[len(side)]
    side_out, p_s = refs[len(side) + 1:-1], refs[-1]
    _run_side_casts(side, side_in, side_out)
    t = pl.program_id(1)
    half = GROUP // 2
    qi = lax.broadcasted_iota(jnp.int32, (WINDOW, 2 * WINDOW), 0)
    kj = lax.broadcasted_iota(jnp.int32, (WINDOW, 2 * WINDOW), 1)
    dist = WINDOW + qi - kj
    valid = (dist >= 0) & (dist < WINDOW)
    before_start = (kj < WINDOW) & (t == 0)
    distf = dist.astype(F32)
    lo = lax.broadcasted_iota(jnp.int32, (1, 2 * HEAD_DIM), 1) < HEAD_DIM
    ones_lo = jnp.where(lo, 1.0, 0.0).astype(BF16)
    ones_hi = jnp.where(lo, 0.0, 1.0).astype(BF16)
    kv = jnp.concatenate([kvp_ref[...], kvc_ref[...]], axis=0)
    nkv = kv.shape[0]
    dk = N_KV_HEADS * HEAD_DIM
    for h in range(N_KV_HEADS):
        pair, odd = divmod(h, 2)
        k_lo, k_hi = _split_halves(kv[:, 128 * pair:128 * pair + 128].astype(F32), odd, lo)
        v_lo, v_hi = _split_halves(kv[:, dk + 128 * pair:dk + 128 * pair + 128].astype(F32), odd, lo)
        v_lo = jnp.concatenate([v_lo, jnp.broadcast_to(ones_lo, (nkv, 2 * HEAD_DIM))], axis=1)
        v_hi = jnp.concatenate([v_hi, jnp.broadcast_to(ones_hi, (nkv, 2 * HEAD_DIM))], axis=1)
        bias = [[jnp.where(valid, slopes[GROUP * h + 2 * pp + par] * distf, jnp.inf) for par in range(2)]
                for pp in range(half)]
        for qb in range(tq // WINDOW):
            rs = slice(WINDOW * qb, WINDOW * qb + 2 * WINDOW)
            kk = jnp.concatenate([k_lo[rs], k_hi[rs]], axis=0)
            vv = jnp.concatenate([v_lo[rs], v_hi[rs]], axis=0)
            ql = jnp.concatenate(
                [q_ref[WINDOW * qb:WINDOW * (qb + 1), 512 * h + 128 * pp:512 * h + 128 * (pp + 1)]
                 for pp in range(half)], axis=0)
            s = lax.dot_general(ql, kk, (((1,), (1,)), ((), ())), preferred_element_type=F32)
            sink_terms = []
            for pp in range(half):
                terms = []
                for par in range(2):
                    sink = sinks_ref[GROUP * h + 2 * pp + par] * LOG2E
                    sp = s[WINDOW * pp:WINDOW * (pp + 1), 2 * WINDOW * par:2 * WINDOW * (par + 1)] - bias[pp][par]
                    if qb == 0:
                        sp = jnp.where(before_start, -jnp.inf, sp)
                    m = jnp.maximum(jnp.max(sp, axis=-1, keepdims=True), sink)
                    p_s[WINDOW * pp:WINDOW * (pp + 1), 2 * WINDOW * par:2 * WINDOW * (par + 1)] = (
                        jnp.exp2(sp - m).astype(BF16))
                    terms.append(jnp.exp2(sink - m))
                sink_terms.append(jnp.where(lo, terms[0], terms[1]))
            pv = jnp.dot(p_s[...], vv, preferred_element_type=F32)
            for pp in range(half):
                rows = slice(WINDOW * pp, WINDOW * (pp + 1))
                den = pv[rows, 2 * HEAD_DIM:] + sink_terms[pp]
                o_ref[WINDOW * qb:WINDOW * (qb + 1), 512 * h + 128 * pp:512 * h + 128 * (pp + 1)] = (
                    pv[rows, :2 * HEAD_DIM] / den).astype(o_ref.dtype)


def _attention(qkv, sinks, bsz, seq, side_jobs=(), *, tq=512):
    n = qkv.shape[0]
    d_attn = N_Q_HEADS * HEAD_DIM
    d_kv2 = 2 * N_KV_HEADS * HEAD_DIM
    nt = seq // tq
    per_w = tq // WINDOW
    kv_col = d_attn // d_kv2
    s_in, s_out, s_shapes = _side_specs(side_jobs, bsz * nt, lambda b, t: b * nt + t)
    return pl.pallas_call(
        functools.partial(_attn_kernel, tq=tq, slopes=_alibi_slopes(), side=_side_splits(side_jobs)),
        grid=(bsz, nt),
        in_specs=[
            pl.BlockSpec(memory_space=pltpu.SMEM),
            pl.BlockSpec((tq, d_attn), lambda b, t: (b * nt + t, 0)),
            pl.BlockSpec((WINDOW, d_kv2),
                         lambda b, t: (b * nt * per_w + jnp.maximum(t * per_w - 1, 0), kv_col)),
            pl.BlockSpec((tq, d_kv2), lambda b, t: (b * nt + t, kv_col)),
            *s_in,
        ],
        out_specs=[pl.BlockSpec((tq, d_attn), lambda b, t: (b * nt + t, 0)), *s_out],
        out_shape=[jax.ShapeDtypeStruct((n, d_attn), BF16), *s_shapes],
        scratch_shapes=[pltpu.VMEM((GROUP // 2 * WINDOW, 4 * WINDOW), BF16)],
        compiler_params=_params(2),
    )(sinks, qkv, qkv, qkv, *[w_ for w_, _ in side_jobs])


def _gate_matmul(xcb, wg_ref, sb):
    wide = wg_ref.shape[1]
    bw = wide // (N_RNN_BLOCKS // RNN_SUPER)
    outs = []
    for j0 in range(0, 2 * wide, MXU_TILE):
        j1 = min(j0 + MXU_TILE, 2 * wide)
        acc = None
        for i0 in range(0, wide, MXU_TILE):
            i1 = min(i0 + MXU_TILE, wide)
            nonzero = any(
                max(bw * b, i0) < min(bw * (b + 1), i1)
                and any(max(off + bw * b, j0) < min(off + bw * (b + 1), j1) for off in (0, wide))
                for b in range(wide // bw))
            if nonzero:
                d = jnp.dot(xcb[:, i0:i1], wg_ref[sb, i0:i1, j0:j1], preferred_element_type=F32)
                acc = d if acc is None else acc + d
        outs.append(acc)
    return jnp.concatenate(outs, axis=1)


def _rnn_kernel(x_ref, wrx_ref, wry_ref, cw_ref, cb_ref, wg_ref, ba_ref, bi_ref, lam_ref, z_ref,
                rx_s, gy_s, xc_s, pa_s, pu_s, hp_s, tail_s, carry_s, *, tt, tn):
    t = pl.program_id(1)
    c_all = z_ref.shape[1]
    csb = c_all // RNN_SUPER
    n_slabs = c_all // LANES
    per_chunk = tn // LANES
    ng = tt // SUBLANES
    width = cw_ref.shape[0]

    @pl.when(t == 0)
    def _():
        carry_s[...] = jnp.zeros_like(carry_s)
        tail_s[...] = jnp.zeros_like(tail_s)

    xb = x_ref[...].astype(BF16)
    for c in range(c_all // tn):
        acc = jnp.dot(xb, wrx_ref[:, c * tn:(c + 1) * tn], preferred_element_type=F32)
        for k in range(per_chunk):
            rx_s[c * per_chunk + k] = acc[:, LANES * k:LANES * (k + 1)]

    def ry_chunk(c):
        ry = jnp.dot(xb, wry_ref[:, c * tn:(c + 1) * tn], preferred_element_type=F32)
        gy = _gelu_tanh(ry)
        for k in range(per_chunk):
            gy_s[c * per_chunk + k] = gy[:, LANES * k:LANES * (k + 1)]

    ry_chunks = list(range(c_all // tn))

    first_row = lax.broadcasted_iota(jnp.int32, (ng, LANES), 0) == 0
    for s in range(n_slabs):
        ls = slice(LANES * s, LANES * (s + 1))
        x_ph = [rx_s[s, pl.ds(j, ng, stride=SUBLANES), :] for j in range(SUBLANES)]
        x_prev = {m: jnp.where(first_row, tail_s[s, m:m + 1, :], pltpu.roll(x_ph[m], 1, axis=0))
                  for m in range(SUBLANES - (width - 1), SUBLANES)}
        for j in range(SUBLANES):
            acc = cb_ref[:, ls]
            for k in range(width):
                src = j - (width - 1) + k
                acc = acc + cw_ref[k:k + 1, ls] * (x_ph[src] if src >= 0 else x_prev[src + SUBLANES])
            xc_s[ng * j:ng * (j + 1), ls] = acc
        tail_s[s] = rx_s[s, tt - SUBLANES:tt, :]

    for sb in range(RNN_SUPER):
        cs = slice(csb * sb, csb * (sb + 1))
        xc = xc_s[:, cs]
        pre = _gate_matmul(xc.astype(BF16), wg_ref, sb)
        n_now = -(-len(ry_chunks) // (RNN_SUPER - sb))
        for c in ry_chunks[:n_now]:
            ry_chunk(c)
        ry_chunks = ry_chunks[n_now:]
        ta = jnp.tanh(pre[:, :csb] + 0.5 * ba_ref[:, cs])
        ti = jnp.tanh(pre[:, csb:] + 0.5 * bi_ref[:, cs])
        half_c_sp = (0.5 * LRU_C) * jax.nn.softplus(-lam_ref[:, cs])
        neg_log_a = ta * half_c_sp + half_c_sp
        a = jnp.exp2(neg_log_a * (-LOG2E))
        hxc = 0.5 * xc
        ix = hxc * ti + hxc
        u = jnp.sqrt(jnp.tanh(neg_log_a) * (a * a + 1.0)) * ix
        pa = a[0:ng]
        pu = u[0:ng]
        pa_s[0:ng, cs] = pa
        pu_s[0:ng, cs] = pu
        for j in range(1, SUBLANES):
            rows = slice(ng * j, ng * (j + 1))
            pu = a[rows] * pu + u[rows]
            pa = a[rows] * pa
            pa_s[rows, cs] = pa
            pu_s[rows, cs] = pu

    last = ng * (SUBLANES - 1)

    def body(g, carry):
        hp_s[pl.ds(g, 1), :] = carry
        return pa_s[pl.ds(last + g, 1), :] * carry + pu_s[pl.ds(last + g, 1), :]

    carry_s[...] = lax.fori_loop(0, ng, body, carry_s[...], unroll=4)

    for s in range(n_slabs):
        ls = slice(LANES * s, LANES * (s + 1))
        hp = hp_s[:, ls]
        for j in range(SUBLANES):
            rows = slice(ng * j, ng * (j + 1))
            h = pa_s[rows, ls] * hp + pu_s[rows, ls]
            rx_s[s, pl.ds(j, ng, stride=SUBLANES), :] = h * gy_s[s, pl.ds(j, ng, stride=SUBLANES), :]
        z_ref[:, ls] = rx_s[s].astype(z_ref.dtype)


def _rnn(x, w_rxy, conv_w, conv_b, wg, ba, bi, lam, bsz, seq, *, tt=256, tn=512):
    n, d = x.shape
    c = w_rxy.shape[1] // 2
    nt = seq // tt
    row = lambda b, t: (b * nt + t, 0)
    vec = lambda a: a.reshape(1, c)
    slab = lambda rows: pltpu.VMEM((c // LANES, rows, LANES), F32)
    return pl.pallas_call(
        functools.partial(_rnn_kernel, tt=tt, tn=tn),
        grid=(bsz, nt),
        in_specs=[
            pl.BlockSpec((tt, d), row),
            pl.BlockSpec((d, c), lambda b, t: (0, 0), pipeline_mode=pl.Buffered(1)),
            pl.BlockSpec((d, c), lambda b, t: (0, 1), pipeline_mode=pl.Buffered(1)),
            _resident(conv_w.shape), _resident((1, c)), _resident(wg.shape),
            _resident((1, c)), _resident((1, c)), _resident((1, c)),
        ],
        out_specs=pl.BlockSpec((tt, c), row),
        out_shape=jax.ShapeDtypeStruct((n, c), BF16),
        scratch_shapes=[
            slab(tt), slab(tt),
            pltpu.VMEM((tt, c), F32), pltpu.VMEM((tt, c), F32), pltpu.VMEM((tt, c), F32),
            pltpu.VMEM((tt // SUBLANES, c), F32),
            slab(SUBLANES),
            pltpu.VMEM((1, c), F32),
        ],
        compiler_params=_params(2),
    )(x, w_rxy, w_rxy, conv_w, vec(conv_b), wg, vec(ba), vec(bi), vec(lam))


def _superblock_kernel(wa_ref, wi_ref, o_ref):
    per, bw, _ = wa_ref.shape
    wide = per * bw
    row = lax.broadcasted_iota(jnp.int32, (bw, wide), 0)
    col = lax.broadcasted_iota(jnp.int32, (bw, wide), 1)
    for i in range(per):
        place = jnp.where(col == row + bw * i, 0.5, 0.0).astype(BF16)
        rs = slice(bw * i, bw * (i + 1))
        o_ref[0, rs, 0:wide] = jnp.dot(wa_ref[i].astype(BF16), place, preferred_element_type=F32).astype(o_ref.dtype)
        o_ref[0, rs, wide:2 * wide] = jnp.dot(wi_ref[i].astype(BF16), place,
                                              preferred_element_type=F32).astype(o_ref.dtype)


def _gate_superblocks(wa, wi):
    nb, bw, _ = wa.shape
    per = nb // RNN_SUPER
    blk = pl.BlockSpec((per, bw, bw), lambda s: (s, 0, 0))
    return pl.pallas_call(
        _superblock_kernel,
        grid=(RNN_SUPER,),
        in_specs=[blk, blk],
        out_specs=pl.BlockSpec((1, per * bw, 2 * per * bw), lambda s: (s, 0, 0)),
        out_shape=jax.ShapeDtypeStruct((RNN_SUPER, per * bw, 2 * per * bw), BF16),
        compiler_params=_params(1),
    )(wa, wi)


def _merge_kernel(o_ref, z_ref, g_ref, *refs, tn, n_split, side):
    wa_refs, wr_refs, refs = refs[:n_split], refs[n_split:2 * n_split], refs[2 * n_split:]
    side_in, m_ref, side_out = refs[:len(side)], refs[len(side)], refs[len(side) + 1:]
    _run_side_casts(side, side_in, side_out)
    o = o_ref[...]
    z = z_ref[...]
    d = m_ref.shape[1]
    for (wa_ref, lc, gc), (wr_ref, _, _) in zip(_col_chunks(wa_refs, tn), _col_chunks(wr_refs, tn)):
        ya = jnp.dot(o, wa_ref[:, lc], preferred_element_type=F32)
        yr = jnp.dot(z, wr_ref[:, lc], preferred_element_type=F32)
        ga = g_ref[:, gc].astype(F32)
        gr = g_ref[:, d + gc.start:d + gc.stop].astype(F32)
        m_ref[:, gc] = (ga * ya + gr * yr).astype(m_ref.dtype)


def _merge(o, z, gates, wa, wr, side_jobs=(), *, tm=512, tn=COL_UNIT):
    n, d = o.shape
    wa_specs, wr_specs = _resident_split(wa.shape), _resident_split(wr.shape)
    assert len(wa_specs) == len(wr_specs)
    s_in, s_out, s_shapes = _side_specs(side_jobs, n // tm, lambda i: i)
    return pl.pallas_call(
        functools.partial(_merge_kernel, tn=tn, n_split=len(wa_specs), side=_side_splits(side_jobs)),
        grid=(n // tm,),
        in_specs=[
            pl.BlockSpec((tm, d), lambda i: (i, 0)),
            pl.BlockSpec((tm, z.shape[1]), lambda i: (i, 0)),
            pl.BlockSpec((tm, 2 * d), lambda i: (i, 0)),
            *wa_specs, *wr_specs, *s_in,
        ],
        out_specs=[pl.BlockSpec((tm, d), lambda i: (i, 0)), *s_out],
        out_shape=[jax.ShapeDtypeStruct((n, d), BF16), *s_shapes],
        compiler_params=_params(1),
    )(o, z, gates, *([wa] * len(wa_specs)), *([wr] * len(wr_specs)), *[w_ for w_, _ in side_jobs])


def _out_ln_kernel(m_ref, x_ref, *refs, tn, alpha):
    *w_refs, g_ref, b_ref, y_ref = refs
    for r in range(m_ref.shape[0] // ROW_SUB):
        rs = slice(r * ROW_SUB, (r + 1) * ROW_SUB)
        m = m_ref[rs, :]
        for w_ref, lc, gc in _col_chunks(w_refs, tn):
            y_ref[rs, gc] = alpha * x_ref[rs, gc] + jnp.dot(m, w_ref[:, lc], preferred_element_type=F32)
        y_ref[rs, :] = _layer_norm(y_ref[rs, :], g_ref[...], b_ref[...])


def _out_ln(m, x, w, g, b, alpha, *, tm=1024, tn=COL_UNIT):
    n, d = x.shape
    w_specs = _resident_split(w.shape)
    return pl.pallas_call(
        functools.partial(_out_ln_kernel, tn=tn, alpha=alpha),
        grid=(n // tm,),
        in_specs=[
            pl.BlockSpec((tm, m.shape[1]), lambda i: (i, 0)),
            pl.BlockSpec((tm, d), lambda i: (i, 0)),
            *w_specs, _resident((1, d)), _resident((1, d)),
        ],
        out_specs=pl.BlockSpec((tm, d), lambda i: (i, 0)),
        out_shape=jax.ShapeDtypeStruct((n, d), F32),
        compiler_params=_params(1),
    )(m, x, *([w] * len(w_specs)), g.reshape(1, d), b.reshape(1, d))


def _ffn_kernel(x_ref, wu_ref, wg_ref, cw_ref, cb_ref, wd_ref, g_ref, b_ref, y_ref,
                xb_s, gs_s, tail_s, *, tm, alpha, blocks_per_seq):
    i = pl.program_id(0)
    c = pl.program_id(1)
    first = (i % blocks_per_seq) == 0

    @pl.when(c == 0)
    def _():
        xb_s[...] = x_ref[...].astype(BF16)
        y_ref[...] = alpha * x_ref[...]

    @pl.when(first)
    def _():
        gs_s[0:HALO] = jnp.zeros((HALO, gs_s.shape[1]), F32)

    @pl.when(jnp.logical_not(first))
    def _():
        gs_s[0:HALO] = tail_s[c]

    xb = xb_s[...]
    gs_s[HALO:HALO + tm] = jnp.dot(xb, wg_ref[...], preferred_element_type=F32)
    tail_s[c] = gs_s[tm:tm + HALO]
    up = jnp.dot(xb, wu_ref[...], preferred_element_type=F32)
    width = cw_ref.shape[0]
    gate = cb_ref[...]
    for k in range(width):
        off = HALO - (width - 1) + k
        gate = gate + cw_ref[k:k + 1, :] * gs_s[off:off + tm, :]
    hid = (jax.nn.gelu(gate, approximate=True) * up).astype(BF16)
    y_ref[...] += jnp.dot(hid, wd_ref[...], preferred_element_type=F32)

    @pl.when(c == pl.num_programs(1) - 1)
    def _():
        y_ref[...] = _layer_norm(y_ref[...], g_ref[...], b_ref[...])


def _ffn(x, wu, wg, conv_w, conv_b, wd, g, b, alpha, seq, *, tm=512, tf=1024):
    n, d = x.shape
    dff = wu.shape[1]
    return pl.pallas_call(
        functools.partial(_ffn_kernel, tm=tm, alpha=alpha, blocks_per_seq=seq // tm),
        grid=(n // tm, dff // tf),
        in_specs=[
            pl.BlockSpec((tm, d), lambda i, c: (i, 0)),
            pl.BlockSpec((d, tf), lambda i, c: (0, c)),
            pl.BlockSpec((d, tf), lambda i, c: (0, c)),
            pl.BlockSpec((conv_w.shape[0], tf), lambda i, c: (0, c)),
            pl.BlockSpec((1, tf), lambda i, c: (0, c)),
            pl.BlockSpec((tf, d), lambda i, c: (c, 0)),
            _resident((1, d)), _resident((1, d)),
        ],
        out_specs=pl.BlockSpec((tm, d), lambda i, c: (i, 0)),
        out_shape=jax.ShapeDtypeStruct((n, d), F32),
        scratch_shapes=[pltpu.VMEM((tm, d), BF16), pltpu.VMEM((HALO + tm, tf), F32),
                        pltpu.VMEM((dff // tf, HALO, tf), F32)],
        compiler_params=_params(2),
    )(x, wu, wg, conv_w, conv_b.reshape(1, dff), wd, g.reshape(1, d), b.reshape(1, d))


def kernel(x, w_in, b_gate, rnn_conv_w, rnn_conv_b, lru_wa, lru_ba, lru_wi, lru_bi, lru_lambda, attn_sinks,
           w_attn_proj, w_rnn_proj, w_out, ln1_g, ln1_b, ffn_w_up, ffn_w_gate, ffn_conv_w, ffn_conv_b,
           ffn_w_down, ln2_g, ln2_b):
    bsz, seq, d = x.shape
    depth = w_in.shape[0]
    alpha = float((2 * depth) ** 0.25)
    d_attn = N_Q_HEADS * HEAD_DIM
    d_qkv = d_attn + 2 * N_KV_HEADS * HEAD_DIM
    d_rnn = rnn_conv_w.shape[-1]

    h = x.reshape(bsz * seq, d)
    for l in range(depth):
        d_in = w_in.shape[-1]
        rest_of_w_in = (w_in[l], ((d_qkv, d_qkv + 2 * d_rnn, 1.0), (d_qkv + 2 * d_rnn, d_in, 0.5)))
        qkv, hb, w_rxy, w_gl_half = _qkv_proj(h, w_in[l], d_qkv, d_attn, HEAD_DIM ** -0.5 * LOG2E, (rest_of_w_in,))
        gates, w_up = _gates_proj(hb, w_gl_half, b_gate[l], (_whole(ffn_w_up[l]),))
        o, w_gate, w_ap, w_rp = _attention(qkv, attn_sinks[l], bsz, seq,
                                           (_whole(ffn_w_gate[l]), _whole(w_attn_proj[l]), _whole(w_rnn_proj[l])))

        wg = _gate_superblocks(lru_wa[l], lru_wi[l])
        z = _rnn(h, w_rxy, rnn_conv_w[l], rnn_conv_b[l], wg, lru_ba[l], lru_bi[l], lru_lambda[l], bsz, seq)

        m, w_down, w_o = _merge(o, z, gates, w_ap, w_rp, (_whole(ffn_w_down[l]), _whole(w_out[l])))
        h = _out_ln(m, h, w_o, ln1_g[l], ln1_b[l], alpha)
        h = _ffn(h, w_up, w_gate, ffn_conv_w[l], ffn_conv_b[l], w_down, ln2_g[l], ln2_b[l], alpha, seq)
    return h.reshape(bsz, seq, d)
```

```python
import functools

import numpy as np
import jax
import jax.numpy as jnp
from jax import lax
from jax.experimental import pallas as pl
from jax.experimental.pallas import tpu as pltpu

F32 = jnp.float32
BF16 = jnp.bfloat16

HEAD_DIM = 64
N_Q_HEADS = 32
N_KV_HEADS = 4
GROUP = N_Q_HEADS // N_KV_HEADS
WINDOW = 128
N_RNN_BLOCKS = 16
RNN_SUPER = 4
LRU_C = 8.0
LN_EPS = 1e-5
LOG2E = 1.4426950408889634
GELU_C0 = 0.7978845608028654
GELU_C1 = GELU_C0 * 0.044715
HALO = 16
SUBLANES = 8
LANES = 128
COL_UNIT = 512
MXU_TILE = 256
ROW_SUB = 512
VMEM_LIMIT = 56 * 1024 * 1024


def _params(n_axes, vmem_limit=VMEM_LIMIT):
    return pltpu.CompilerParams(dimension_semantics=("arbitrary",) * n_axes,
                                vmem_limit_bytes=vmem_limit)


def _resident(shape):
    nd = len(shape)
    return pl.BlockSpec(shape, lambda *_: (0,) * nd, pipeline_mode=pl.Buffered(1))


def _split_cols(width):
    if width % (2 * COL_UNIT):
        return [(width, 0)]
    return [(width - COL_UNIT, 0), (COL_UNIT, width - COL_UNIT)]


def _resident_split(shape):
    rows, width = shape
    return [pl.BlockSpec((rows, w), functools.partial(lambda j, *_: (0, j), off // w), pipeline_mode=pl.Buffered(1))
            for w, off in _split_cols(width)]


def _col_chunks(refs, tn):
    out, off = [], 0
    for r in refs:
        for c in range(r.shape[1] // tn):
            out.append((r, slice(c * tn, (c + 1) * tn), slice(off, off + tn)))
            off += tn
    return out


def _gelu_tanh(x):
    hx = 0.5 * x
    return hx * jnp.tanh(x * (GELU_C0 + GELU_C1 * (x * x))) + hx


def _layer_norm(v, g, b):
    mu = jnp.mean(v, axis=-1, keepdims=True)
    d = v - mu
    var = jnp.mean(d * d, axis=-1, keepdims=True)
    return d * lax.rsqrt(var + LN_EPS) * g + b


def _side_specs(jobs, steps, row_index):
    in_specs, out_specs, out_shapes = [], [], []
    for w, splits in jobs:
        rows, cols = w.shape
        tr = rows // steps
        assert tr * steps == rows and tr % 16 == 0
        in_specs.append(pl.BlockSpec((tr, cols), lambda *g: (row_index(*g), 0)))
        for lo, hi, _ in splits:
            out_specs.append(pl.BlockSpec((tr, hi - lo), lambda *g: (row_index(*g), 0)))
            out_shapes.append(jax.ShapeDtypeStruct((rows, hi - lo), BF16))
    return in_specs, out_specs, out_shapes


def _side_splits(jobs):
    return tuple(splits for _, splits in jobs)


def _run_side_casts(side, in_refs, out_refs):
    out_refs = list(out_refs)
    for splits, w_ref in zip(side, in_refs):
        for lo, hi, scale in splits:
            w = w_ref[:, lo:hi]
            out_refs.pop(0)[...] = (w if scale == 1.0 else w * scale).astype(BF16)


def _whole(w):
    return (w, ((0, w.shape[1], 1.0),))


def _qkv_kernel(x_ref, w_ref, *refs, tn, n_scaled, scale, side):
    side_in, o_ref, side_out = refs[:len(side)], refs[len(side)], refs[len(side) + 1:]
    _run_side_casts(side, side_in, side_out)
    for r in range(x_ref.shape[0] // ROW_SUB):
        rs = slice(r * ROW_SUB, (r + 1) * ROW_SUB)
        xb = x_ref[rs, :].astype(BF16)
        for c in range(w_ref.shape[1] // tn):
            cs = slice(c * tn, (c + 1) * tn)
            acc = jnp.dot(xb, w_ref[:, cs].astype(BF16), preferred_element_type=F32)
            if c < n_scaled:
                acc = acc * scale
            o_ref[rs, cs] = acc.astype(o_ref.dtype)


def _qkv_proj(x, w, nc, d_attn, scale, side_jobs=(), *, tm=512, tn=512):
    n, d = x.shape
    s_in, s_out, s_shapes = _side_specs(side_jobs, n // tm, lambda i: i)
    return pl.pallas_call(
        functools.partial(_qkv_kernel, tn=tn, n_scaled=d_attn // tn, scale=scale, side=_side_splits(side_jobs)),
        grid=(n // tm,),
        in_specs=[pl.BlockSpec((tm, d), lambda i: (i, 0)), _resident((d, nc)), *s_in],
        out_specs=[pl.BlockSpec((tm, nc), lambda i: (i, 0)), *s_out],
        out_shape=[jax.ShapeDtypeStruct((n, nc), BF16), *s_shapes],
        compiler_params=_params(1),
    )(x, w, *[w_ for w_, _ in side_jobs])


def _gates_kernel(x_ref, *refs, tn, n_w, side):
    w_refs, b_ref = refs[:n_w], refs[n_w]
    side_in, o_ref, side_out = refs[n_w + 1:n_w + 1 + len(side)], refs[n_w + 1 + len(side)], refs[n_w + 2 + len(side):]
    _run_side_casts(side, side_in, side_out)
    for r in range(x_ref.shape[0] // ROW_SUB):
        rs = slice(r * ROW_SUB, (r + 1) * ROW_SUB)
        xb = x_ref[rs, :].astype(BF16)
        for w_ref, lc, gc in _col_chunks(w_refs, tn):
            half_z = jnp.dot(xb, w_ref[:, lc], preferred_element_type=F32) + 0.5 * b_ref[:, gc]
            o_ref[rs, gc] = (0.5 * jnp.tanh(half_z) + 0.5).astype(o_ref.dtype)


def _gates_proj(x, w, bias, side_jobs=(), *, tm=512, tn=COL_UNIT):
    n, d = x.shape
    nc = w.shape[1]
    w_specs = _resident_split((d, nc))
    s_in, s_out, s_shapes = _side_specs(side_jobs, n // tm, lambda i: i)
    return pl.pallas_call(
        functools.partial(_gates_kernel, tn=tn, n_w=len(w_specs), side=_side_splits(side_jobs)),
        grid=(n // tm,),
        in_specs=[pl.BlockSpec((tm, d), lambda i: (i, 0)), *w_specs, _resident((1, nc)), *s_in],
        out_specs=[pl.BlockSpec((tm, nc), lambda i: (i, 0)), *s_out],
        out_shape=[jax.ShapeDtypeStruct((n, nc), BF16), *s_shapes],
        compiler_params=_params(1),
    )(x, *([w] * len(w_specs)), bias.reshape(1, nc), *[w_ for w_, _ in side_jobs])


def _alibi_slopes():
    h = np.arange(1, N_Q_HEADS + 1, dtype=np.float32)
    return [float(s) * LOG2E for s in (2.0 ** (-8.0 * h / N_Q_HEADS)).astype(np.float32)]


def _split_halves(xp, odd, lo):
    xr = pltpu.roll(xp, HEAD_DIM, axis=1)
    zero = jnp.zeros_like(xp)
    if odd:
        return jnp.where(lo, xr, zero).astype(BF16), jnp.where(lo, zero, xp).astype(BF16)
    return jnp.where(lo, xp, zero).astype(BF16), jnp.where(lo, zero, xr).astype(BF16)


def _attn_kernel(sinks_ref, q_ref, kvp_ref, kvc_ref, *refs, tq, slopes, side):
    side_in, o_ref = refs[:len(side)], refs[len(side)]
    side_out, p_s = refs[len(side) + 1:-1], refs[-1]
    _run_side_casts(side, side_in, side_out)
    t = pl.program_id(1)
    half = GROUP // 2
    qi = lax.broadcasted_iota(jnp.int32, (WINDOW, 2 * WINDOW), 0)
    kj = lax.broadcasted_iota(jnp.int32, (WINDOW, 2 * WINDOW), 1)
    dist = WINDOW + qi - kj
    valid = (dist >= 0) & (dist < WINDOW)
    before_start = (kj < WINDOW) & (t == 0)
    distf = dist.astype(F32)
    lo = lax.broadcasted_iota(jnp.int32, (1, 2 * HEAD_DIM), 1) < HEAD_DIM
    ones_lo = jnp.where(lo, 1.0, 0.0).astype(BF16)
    ones_hi = jnp.where(lo, 0.0, 1.0).astype(BF16)
    kv = jnp.concatenate([kvp_ref[...], kvc_ref[...]], axis=0)
    nkv = kv.shape[0]
    dk = N_KV_HEADS * HEAD_DIM
    for h in range(N_KV_HEADS):
        pair, odd = divmod(h, 2)
        k_lo, k_hi = _split_halves(kv[:, 128 * pair:128 * pair + 128].astype(F32), odd, lo)
        v_lo, v_hi = _split_halves(kv[:, dk + 128 * pair:dk + 128 * pair + 128].astype(F32), odd, lo)
        v_lo = jnp.concatenate([v_lo, jnp.broadcast_to(ones_lo, (nkv, 2 * HEAD_DIM))], axis=1)
        v_hi = jnp.concatenate([v_hi, jnp.broadcast_to(ones_hi, (nkv, 2 * HEAD_DIM))], axis=1)
        bias = [[jnp.where(valid, slopes[GROUP * h + 2 * pp + par] * distf, jnp.inf) for par in range(2)]
                for pp in range(half)]
        for qb in range(tq // WINDOW):
            rs = slice(WINDOW * qb, WINDOW * qb + 2 * WINDOW)
            kk = jnp.concatenate([k_lo[rs], k_hi[rs]], axis=0)
            vv = jnp.concatenate([v_lo[rs], v_hi[rs]], axis=0)
            ql = jnp.concatenate(
                [q_ref[WINDOW * qb:WINDOW * (qb + 1), 512 * h + 128 * pp:512 * h + 128 * (pp + 1)]
                 for pp in range(half)], axis=0)
            s = lax.dot_general(ql, kk, (((1,), (1,)), ((), ())), preferred_element_type=F32)
            sink_terms = []
            for pp in range(half):
                terms = []
                for par in range(2):
                    sink = sinks_ref[GROUP * h + 2 * pp + par] * LOG2E
                    sp = s[WINDOW * pp:WINDOW * (pp + 1), 2 * WINDOW * par:2 * WINDOW * (par + 1)] - bias[pp][par]
                    if qb == 0:
                        sp = jnp.where(before_start, -jnp.inf, sp)
                    m = jnp.maximum(jnp.max(sp, axis=-1, keepdims=True), sink)
                    p_s[WINDOW * pp:WINDOW * (pp + 1), 2 * WINDOW * par:2 * WINDOW * (par + 1)] = (
                        jnp.exp2(sp - m).astype(BF16))
                    terms.append(jnp.exp2(sink - m))
                sink_terms.append(jnp.where(lo, terms[0], terms[1]))
            pv = jnp.dot(p_s[...], vv, preferred_element_type=F32)
            for pp in range(half):
                rows = slice(WINDOW * pp, WINDOW * (pp + 1))
                den = pv[rows, 2 * HEAD_DIM:] + sink_terms[pp]
                o_ref[WINDOW * qb:WINDOW * (qb + 1), 512 * h + 128 * pp:512 * h + 128 * (pp + 1)] = (
                    pv[rows, :2 * HEAD_DIM] / den).astype(o_ref.dtype)


def _attention(qkv, sinks, bsz, seq, side_jobs=(), *, tq=512):
    n = qkv.shape[0]
    d_attn = N_Q_HEADS * HEAD_DIM
    d_kv2 = 2 * N_KV_HEADS * HEAD_DIM
    nt = seq // tq
    per_w = tq // WINDOW
    kv_col = d_attn // d_kv2
    s_in, s_out, s_shapes = _side_specs(side_jobs, bsz * nt, lambda b, t: b * nt + t)
    return pl.pallas_call(
        functools.partial(_attn_kernel, tq=tq, slopes=_alibi_slopes(), side=_side_splits(side_jobs)),
        grid=(bsz, nt),
        in_specs=[
            pl.BlockSpec(memory_space=pltpu.SMEM),
            pl.BlockSpec((tq, d_attn), lambda b, t: (b * nt + t, 0)),
            pl.BlockSpec((WINDOW, d_kv2),
                         lambda b, t: (b * nt * per_w + jnp.maximum(t * per_w - 1, 0), kv_col)),
            pl.BlockSpec((tq, d_kv2), lambda b, t: (b * nt + t, kv_col)),
            *s_in,
        ],
        out_specs=[pl.BlockSpec((tq, d_attn), lambda b, t: (b * nt + t, 0)), *s_out],
        out_shape=[jax.ShapeDtypeStruct((n, d_attn), BF16), *s_shapes],
        scratch_shapes=[pltpu.VMEM((GROUP // 2 * WINDOW, 4 * WINDOW), BF16)],
        compiler_params=_params(2),
    )(sinks, qkv, qkv, qkv, *[w_ for w_, _ in side_jobs])


def _gate_matmul(xcb, wg_ref, sb):
    wide = wg_ref.shape[1]
    bw = wide // (N_RNN_BLOCKS // RNN_SUPER)
    outs = []
    for j0 in range(0, 2 * wide, MXU_TILE):
        j1 = min(j0 + MXU_TILE, 2 * wide)
        acc = None
        for i0 in range(0, wide, MXU_TILE):
            i1 = min(i0 + MXU_TILE, wide)
            nonzero = any(
                max(bw * b, i0) < min(bw * (b + 1), i1)
                and any(max(off + bw * b, j0) < min(off + bw * (b + 1), j1) for off in (0, wide))
                for b in range(wide // bw))
            if nonzero:
                d = jnp.dot(xcb[:, i0:i1], wg_ref[sb, i0:i1, j0:j1], preferred_element_type=F32)
                acc = d if acc is None else acc + d
        outs.append(acc)
    return jnp.concatenate(outs, axis=1)


def _rnn_kernel(x_ref, wrx_ref, wry_ref, cw_ref, cb_ref, wg_ref, ba_ref, bi_ref, lam_ref, z_ref,
                rx_s, gy_s, xc_s, pa_s, pu_s, hp_s, tail_s, carry_s, *, tt, tn):
    t = pl.program_id(1)
    c_all = z_ref.shape[1]
    csb = c_all // RNN_SUPER
    n_slabs = c_all // LANES
    per_chunk = tn // LANES
    ng = tt // SUBLANES
    width = cw_ref.shape[0]

    @pl.when(t == 0)
    def _():
        carry_s[...] = jnp.zeros_like(carry_s)
        tail_s[...] = jnp.zeros_like(tail_s)

    xb = x_ref[...].astype(BF16)
    for c in range(c_all // tn):
        acc = jnp.dot(xb, wrx_ref[:, c * tn:(c + 1) * tn], preferred_element_type=F32)
        for k in range(per_chunk):
            rx_s[c * per_chunk + k] = acc[:, LANES * k:LANES * (k + 1)]

    def ry_chunk(c):
        ry = jnp.dot(xb, wry_ref[:, c * tn:(c + 1) * tn], preferred_element_type=F32)
        gy = _gelu_tanh(ry)
        for k in range(per_chunk):
            gy_s[c * per_chunk + k] = gy[:, LANES * k:LANES * (k + 1)]

    ry_chunks = list(range(c_all // tn))

    first_row = lax.broadcasted_iota(jnp.int32, (ng, LANES), 0) == 0
    for s in range(n_slabs):
        ls = slice(LANES * s, LANES * (s + 1))
        x_ph = [rx_s[s, pl.ds(j, ng, stride=SUBLANES), :] for j in range(SUBLANES)]
        x_prev = {m: jnp.where(first_row, tail_s[s, m:m + 1, :], pltpu.roll(x_ph[m], 1, axis=0))
                  for m in range(SUBLANES - (width - 1), SUBLANES)}
        for j in range(SUBLANES):
            acc = cb_ref[:, ls]
            for k in range(width):
                src = j - (width - 1) + k
                acc = acc + cw_ref[k:k + 1, ls] * (x_ph[src] if src >= 0 else x_prev[src + SUBLANES])
            xc_s[ng * j:ng * (j + 1), ls] = acc
        tail_s[s] = rx_s[s, tt - SUBLANES:tt, :]

    for sb in range(RNN_SUPER):
        cs = slice(csb * sb, csb * (sb + 1))
        xc = xc_s[:, cs]
        pre = _gate_matmul(xc.astype(BF16), wg_ref, sb)
        n_now = -(-len(ry_chunks) // (RNN_SUPER - sb))
        for c in ry_chunks[:n_now]:
            ry_chunk(c)
        ry_chunks = ry_chunks[n_now:]
        ta = jnp.tanh(pre[:, :csb] + 0.5 * ba_ref[:, cs])
        ti = jnp.tanh(pre[:, csb:] + 0.5 * bi_ref[:, cs])
        half_c_sp = (0.5 * LRU_C) * jax.nn.softplus(-lam_ref[:, cs])
        neg_log_a = ta * half_c_sp + half_c_sp
        a = jnp.exp2(neg_log_a * (-LOG2E))
        hxc = 0.5 * xc
        ix = hxc * ti + hxc
        u = jnp.sqrt(jnp.tanh(neg_log_a) * (a * a + 1.0)) * ix
        pa = a[0:ng]
        pu = u[0:ng]
        pa_s[0:ng, cs] = pa
        pu_s[0:ng, cs] = pu
        for j in range(1, SUBLANES):
            rows = slice(ng * j, ng * (j + 1))
            pu = a[rows] * pu + u[rows]
            pa = a[rows] * pa
            pa_s[rows, cs] = pa
            pu_s[rows, cs] = pu

    last = ng * (SUBLANES - 1)

    def body(g, carry):
        hp_s[pl.ds(g, 1), :] = carry
        return pa_s[pl.ds(last + g, 1), :] * carry + pu_s[pl.ds(last + g, 1), :]

    carry_s[...] = lax.fori_loop(0, ng, body, carry_s[...], unroll=4)

    for s in range(n_slabs):
        ls = slice(LANES * s, LANES * (s + 1))
        hp = hp_s[:, ls]
        for j in range(SUBLANES):
            rows = slice(ng * j, ng * (j + 1))
            h = pa_s[rows, ls] * hp + pu_s[rows, ls]
            rx_s[s, pl.ds(j, ng, stride=SUBLANES), :] = h * gy_s[s, pl.ds(j, ng, stride=SUBLANES), :]
        z_ref[:, ls] = rx_s[s].astype(z_ref.dtype)


def _rnn(x, w_rxy, conv_w, conv_b, wg, ba, bi, lam, bsz, seq, *, tt=256, tn=512):
    n, d = x.shape
    c = w_rxy.shape[1] // 2
    nt = seq // tt
    row = lambda b, t: (b * nt + t, 0)
    vec = lambda a: a.reshape(1, c)
    slab = lambda rows: pltpu.VMEM((c // LANES, rows, LANES), F32)
    return pl.pallas_call(
        functools.partial(_rnn_kernel, tt=tt, tn=tn),
        grid=(bsz, nt),
        in_specs=[
            pl.BlockSpec((tt, d), row),
            pl.BlockSpec((d, c), lambda b, t: (0, 0), pipeline_mode=pl.Buffered(1)),
            pl.BlockSpec((d, c), lambda b, t: (0, 1), pipeline_mode=pl.Buffered(1)),
            _resident(conv_w.shape), _resident((1, c)), _resident(wg.shape),
            _resident((1, c)), _resident((1, c)), _resident((1, c)),
        ],
        out_specs=pl.BlockSpec((tt, c), row),
        out_shape=jax.ShapeDtypeStruct((n, c), BF16),
        scratch_shapes=[
            slab(tt), slab(tt),
            pltpu.VMEM((tt, c), F32), pltpu.VMEM((tt, c), F32), pltpu.VMEM((tt, c), F32),
            pltpu.VMEM((tt // SUBLANES, c), F32),
            slab(SUBLANES),
            pltpu.VMEM((1, c), F32),
        ],
        compiler_params=_params(2),
    )(x, w_rxy, w_rxy, conv_w, vec(conv_b), wg, vec(ba), vec(bi), vec(lam))


def _superblock_kernel(wa_ref, wi_ref, o_ref):
    per, bw, _ = wa_ref.shape
    wide = per * bw
    row = lax.broadcasted_iota(jnp.int32, (bw, wide), 0)
    col = lax.broadcasted_iota(jnp.int32, (bw, wide), 1)
    for i in range(per):
        place = jnp.where(col == row + bw * i, 0.5, 0.0).astype(BF16)
        rs = slice(bw * i, bw * (i + 1))
        o_ref[0, rs, 0:wide] = jnp.dot(wa_ref[i].astype(BF16), place, preferred_element_type=F32).astype(o_ref.dtype)
        o_ref[0, rs, wide:2 * wide] = jnp.dot(wi_ref[i].astype(BF16), place,
                                              preferred_element_type=F32).astype(o_ref.dtype)


def _gate_superblocks(wa, wi):
    nb, bw, _ = wa.shape
    per = nb // RNN_SUPER
    blk = pl.BlockSpec((per, bw, bw), lambda s: (s, 0, 0))
    return pl.pallas_call(
        _superblock_kernel,
        grid=(RNN_SUPER,),
        in_specs=[blk, blk],
        out_specs=pl.BlockSpec((1, per * bw, 2 * per * bw), lambda s: (s, 0, 0)),
        out_shape=jax.ShapeDtypeStruct((RNN_SUPER, per * bw, 2 * per * bw), BF16),
        compiler_params=_params(1),
    )(wa, wi)


def _merge_kernel(o_ref, z_ref, g_ref, *refs, tn, n_split, side):
    wa_refs, wr_refs, refs = refs[:n_split], refs[n_split:2 * n_split], refs[2 * n_split:]
    side_in, m_ref, side_out = refs[:len(side)], refs[len(side)], refs[len(side) + 1:]
    _run_side_casts(side, side_in, side_out)
    o = o_ref[...]
    z = z_ref[...]
    d = m_ref.shape[1]
    for (wa_ref, lc, gc), (wr_ref, _, _) in zip(_col_chunks(wa_refs, tn), _col_chunks(wr_refs, tn)):
        ya = jnp.dot(o, wa_ref[:, lc], preferred_element_type=F32)
        yr = jnp.dot(z, wr_ref[:, lc], preferred_element_type=F32)
        ga = g_ref[:, gc].astype(F32)
        gr = g_ref[:, d + gc.start:d + gc.stop].astype(F32)
        m_ref[:, gc] = (ga * ya + gr * yr).astype(m_ref.dtype)


def _merge(o, z, gates, wa, wr, side_jobs=(), *, tm=512, tn=COL_UNIT):
    n, d = o.shape
    wa_specs, wr_specs = _resident_split(wa.shape), _resident_split(wr.shape)
    assert len(wa_specs) == len(wr_specs)
    s_in, s_out, s_shapes = _side_specs(side_jobs, n // tm, lambda i: i)
    return pl.pallas_call(
        functools.partial(_merge_kernel, tn=tn, n_split=len(wa_specs), side=_side_splits(side_jobs)),
        grid=(n // tm,),
        in_specs=[
            pl.BlockSpec((tm, d), lambda i: (i, 0)),
            pl.BlockSpec((tm, z.shape[1]), lambda i: (i, 0)),
            pl.BlockSpec((tm, 2 * d), lambda i: (i, 0)),
            *wa_specs, *wr_specs, *s_in,
        ],
        out_specs=[pl.BlockSpec((tm, d), lambda i: (i, 0)), *s_out],
        out_shape=[jax.ShapeDtypeStruct((n, d), BF16), *s_shapes],
        compiler_params=_params(1),
    )(o, z, gates, *([wa] * len(wa_specs)), *([wr] * len(wr_specs)), *[w_ for w_, _ in side_jobs])


def _out_ln_kernel(m_ref, x_ref, *refs, tn, alpha):
    *w_refs, g_ref, b_ref, y_ref = refs
    for r in range(m_ref.shape[0] // ROW_SUB):
        rs = slice(r * ROW_SUB, (r + 1) * ROW_SUB)
        m = m_ref[rs, :]
        for w_ref, lc, gc in _col_chunks(w_refs, tn):
            y_ref[rs, gc] = alpha * x_ref[rs, gc] + jnp.dot(m, w_ref[:, lc], preferred_element_type=F32)
        y_ref[rs, :] = _layer_norm(y_ref[rs, :], g_ref[...], b_ref[...])


def _out_ln(m, x, w, g, b, alpha, *, tm=1024, tn=COL_UNIT):
    n, d = x.shape
    w_specs = _resident_split(w.shape)
    return pl.pallas_call(
        functools.partial(_out_ln_kernel, tn=tn, alpha=alpha),
        grid=(n // tm,),
        in_specs=[
            pl.BlockSpec((tm, m.shape[1]), lambda i: (i, 0)),
            pl.BlockSpec((tm, d), lambda i: (i, 0)),
            *w_specs, _resident((1, d)), _resident((1, d)),
        ],
        out_specs=pl.BlockSpec((tm, d), lambda i: (i, 0)),
        out_shape=jax.ShapeDtypeStruct((n, d), F32),
        compiler_params=_params(1),
    )(m, x, *([w] * len(w_specs)), g.reshape(1, d), b.reshape(1, d))


def _ffn_kernel(x_ref, wu_ref, wg_ref, cw_ref, cb_ref, wd_ref, g_ref, b_ref, y_ref,
                xb_s, gs_s, tail_s, *, tm, alpha, blocks_per_seq):
    i = pl.program_id(0)
    c = pl.program_id(1)
    first = (i % blocks_per_seq) == 0

    @pl.when(c == 0)
    def _():
        xb_s[...] = x_ref[...].astype(BF16)
        y_ref[...] = alpha * x_ref[...]

    @pl.when(first)
    def _():
        gs_s[0:HALO] = jnp.zeros((HALO, gs_s.shape[1]), F32)

    @pl.when(jnp.logical_not(first))
    def _():
        gs_s[0:HALO] = tail_s[c]

    xb = xb_s[...]
    gs_s[HALO:HALO + tm] = jnp.dot(xb, wg_ref[...], preferred_element_type=F32)
    tail_s[c] = gs_s[tm:tm + HALO]
    up = jnp.dot(xb, wu_ref[...], preferred_element_type=F32)
    width = cw_ref.shape[0]
    gate = cb_ref[...]
    for k in range(width):
        off = HALO - (width - 1) + k
        gate = gate + cw_ref[k:k + 1, :] * gs_s[off:off + tm, :]
    hid = (jax.nn.gelu(gate, approximate=True) * up).astype(BF16)
    y_ref[...] += jnp.dot(hid, wd_ref[...], preferred_element_type=F32)

    @pl.when(c == pl.num_programs(1) - 1)
    def _():
        y_ref[...] = _layer_norm(y_ref[...], g_ref[...], b_ref[...])


def _ffn(x, wu, wg, conv_w, conv_b, wd, g, b, alpha, seq, *, tm=512, tf=1024):
    n, d = x.shape
    dff = wu.shape[1]
    return pl.pallas_call(
        functools.partial(_ffn_kernel, tm=tm, alpha=alpha, blocks_per_seq=seq // tm),
        grid=(n // tm, dff // tf),
        in_specs=[
            pl.BlockSpec((tm, d), lambda i, c: (i, 0)),
            pl.BlockSpec((d, tf), lambda i, c: (0, c)),
            pl.BlockSpec((d, tf), lambda i, c: (0, c)),
            pl.BlockSpec((conv_w.shape[0], tf), lambda i, c: (0, c)),
            pl.BlockSpec((1, tf), lambda i, c: (0, c)),
            pl.BlockSpec((tf, d), lambda i, c: (c, 0)),
            _resident((1, d)), _resident((1, d)),
        ],
        out_specs=pl.BlockSpec((tm, d), lambda i, c: (i, 0)),
        out_shape=jax.ShapeDtypeStruct((n, d), F32),
        scratch_shapes=[pltpu.VMEM((tm, d), BF16), pltpu.VMEM((HALO + tm, tf), F32),
                        pltpu.VMEM((dff // tf, HALO, tf), F32)],
        compiler_params=_params(2),
    )(x, wu, wg, conv_w, conv_b.reshape(1, dff), wd, g.reshape(1, d), b.reshape(1, d))


def kernel(x, w_in, b_gate, rnn_conv_w, rnn_conv_b, lru_wa, lru_ba, lru_wi, lru_bi, lru_lambda, attn_sinks,
           w_attn_proj, w_rnn_proj, w_out, ln1_g, ln1_b, ffn_w_up, ffn_w_gate, ffn_conv_w, ffn_conv_b,
           ffn_w_down, ln2_g, ln2_b):
    bsz, seq, d = x.shape
    depth = w_in.shape[0]
    alpha = float((2 * depth) ** 0.25)
    d_attn = N_Q_HEADS * HEAD_DIM
    d_qkv = d_attn + 2 * N_KV_HEADS * HEAD_DIM
    d_rnn = rnn_conv_w.shape[-1]

    h = x.reshape(bsz * seq, d)
    for l in range(depth):
        d_in = w_in.shape[-1]
        rest_of_w_in = (w_in[l], ((d_qkv, d_qkv + 2 * d_rnn, 1.0), (d_qkv + 2 * d_rnn, d_in, 0.5)))
        qkv, w_rxy, w_gl_half = _qkv_proj(h, w_in[l], d_qkv, d_attn, HEAD_DIM ** -0.5 * LOG2E, (rest_of_w_in,))
        gates, w_up = _gates_proj(h, w_gl_half, b_gate[l], (_whole(ffn_w_up[l]),))
        o, w_gate, w_ap, w_rp = _attention(qkv, attn_sinks[l], bsz, seq,
                                           (_whole(ffn_w_gate[l]), _whole(w_attn_proj[l]), _whole(w_rnn_proj[l])))

        wg = _gate_superblocks(lru_wa[l], lru_wi[l])
        z = _rnn(h, w_rxy, rnn_conv_w[l], rnn_conv_b[l], wg, lru_ba[l], lru_bi[l], lru_lambda[l], bsz, seq)

        m, w_down, w_o = _merge(o, z, gates, w_ap, w_rp, (_whole(ffn_w_down[l]), _whole(w_out[l])))
        h = _out_ln(m, h, w_o, ln1_g[l], ln1_b[l], alpha)
        h = _ffn(h, w_up, w_gate, ffn_conv_w[l], ffn_conv_b[l], w_down, ln2_g[l], ln2_b[l], alpha, seq)
    return h.reshape(bsz, seq, d)
```

```python
import functools

import numpy as np
import jax
import jax.numpy as jnp
from jax import lax
from jax.experimental import pallas as pl
from jax.experimental.pallas import tpu as pltpu

F32 = jnp.float32
BF16 = jnp.bfloat16

HEAD_DIM = 64
N_Q_HEADS = 32
N_KV_HEADS = 4
GROUP = N_Q_HEADS // N_KV_HEADS
WINDOW = 128
N_RNN_BLOCKS = 16
RNN_SUPER = 4
LRU_C = 8.0
LN_EPS = 1e-5
LOG2E = 1.4426950408889634
GELU_C0 = 0.7978845608028654
GELU_C1 = GELU_C0 * 0.044715
HALO = 16
SUBLANES = 8
LANES = 128
COL_UNIT = 512
MXU_TILE = 256
ROW_SUB = 512
BF16_ROWS = 16
VMEM_LIMIT = 56 * 1024 * 1024


def _params(n_axes, vmem_limit=VMEM_LIMIT):
    return pltpu.CompilerParams(dimension_semantics=("arbitrary",) * n_axes,
                                vmem_limit_bytes=vmem_limit)


def _resident(shape):
    nd = len(shape)
    return pl.BlockSpec(shape, lambda *_: (0,) * nd, pipeline_mode=pl.Buffered(1))


def _split_cols(width):
    if width % (2 * COL_UNIT):
        return [(width, 0)]
    return [(width - COL_UNIT, 0), (COL_UNIT, width - COL_UNIT)]


def _resident_split(shape):
    rows, width = shape
    return [pl.BlockSpec((rows, w), functools.partial(lambda j, *_: (0, j), off // w), pipeline_mode=pl.Buffered(1))
            for w, off in _split_cols(width)]


def _col_chunks(refs, tn):
    out, off = [], 0
    for r in refs:
        for c in range(r.shape[1] // tn):
            out.append((r, slice(c * tn, (c + 1) * tn), slice(off, off + tn)))
            off += tn
    return out


def _gelu_tanh(x):
    hx = 0.5 * x
    return hx * jnp.tanh(x * (GELU_C0 + GELU_C1 * (x * x))) + hx


def _layer_norm(v, g, b):
    mu = jnp.mean(v, axis=-1, keepdims=True)
    d = v - mu
    var = jnp.mean(d * d, axis=-1, keepdims=True)
    return d * lax.rsqrt(var + LN_EPS) * g + b


def _side_specs(jobs, steps, row_index):
    in_specs, out_specs, out_shapes = [], [], []
    for w, splits in jobs:
        rows, cols = w.shape
        tr = rows // steps
        assert tr * steps == rows and tr % BF16_ROWS == 0
        in_specs.append(pl.BlockSpec((tr, cols), lambda *g: (row_index(*g), 0)))
        for lo, hi, _ in splits:
            out_specs.append(pl.BlockSpec((tr, hi - lo), lambda *g: (row_index(*g), 0)))
            out_shapes.append(jax.ShapeDtypeStruct((rows, hi - lo), BF16))
    return in_specs, out_specs, out_shapes


def _side_splits(jobs):
    return tuple(splits for _, splits in jobs)


def _run_side_casts(side, in_refs, out_refs):
    out_refs = list(out_refs)
    for splits, w_ref in zip(side, in_refs):
        for lo, hi, scale in splits:
            w = w_ref[:, lo:hi]
            out_refs.pop(0)[...] = (w if scale == 1.0 else w * scale).astype(BF16)


def _whole(w):
    return (w, ((0, w.shape[1], 1.0),))


def _qkv_kernel(x_ref, w_ref, *refs, tn, n_scaled, scale, side):
    side_in, (o_ref, xb_ref), side_out = refs[:len(side)], refs[len(side):len(side) + 2], refs[len(side) + 2:]
    _run_side_casts(side, side_in, side_out)
    for r in range(x_ref.shape[0] // ROW_SUB):
        rs = slice(r * ROW_SUB, (r + 1) * ROW_SUB)
        xb = x_ref[rs, :].astype(BF16)
        xb_ref[rs, :] = xb
        for c in range(w_ref.shape[1] // tn):
            cs = slice(c * tn, (c + 1) * tn)
            acc = jnp.dot(xb, w_ref[:, cs].astype(BF16), preferred_element_type=F32)
            if c < n_scaled:
                acc = acc * scale
            o_ref[rs, cs] = acc.astype(o_ref.dtype)


def _qkv_proj(x, w, nc, d_attn, scale, side_jobs=(), *, tm=512, tn=512):
    n, d = x.shape
    s_in, s_out, s_shapes = _side_specs(side_jobs, n // tm, lambda i: i)
    return pl.pallas_call(
        functools.partial(_qkv_kernel, tn=tn, n_scaled=d_attn // tn, scale=scale, side=_side_splits(side_jobs)),
        grid=(n // tm,),
        in_specs=[pl.BlockSpec((tm, d), lambda i: (i, 0)), _resident((d, nc)), *s_in],
        out_specs=[pl.BlockSpec((tm, nc), lambda i: (i, 0)), pl.BlockSpec((tm, d), lambda i: (i, 0)), *s_out],
        out_shape=[jax.ShapeDtypeStruct((n, nc), BF16), jax.ShapeDtypeStruct((n, d), BF16), *s_shapes],
        compiler_params=_params(1),
    )(x, w, *[w_ for w_, _ in side_jobs])


def _gates_kernel(x_ref, *refs, tn, n_w, side):
    w_refs, b_ref = refs[:n_w], refs[n_w]
    side_in, o_ref, side_out = refs[n_w + 1:n_w + 1 + len(side)], refs[n_w + 1 + len(side)], refs[n_w + 2 + len(side):]
    _run_side_casts(side, side_in, side_out)
    for r in range(x_ref.shape[0] // ROW_SUB):
        rs = slice(r * ROW_SUB, (r + 1) * ROW_SUB)
        xb = x_ref[rs, :]
        for w_ref, lc, gc in _col_chunks(w_refs, tn):
            half_z = jnp.dot(xb, w_ref[:, lc], preferred_element_type=F32) + 0.5 * b_ref[:, gc]
            o_ref[rs, gc] = (0.5 * jnp.tanh(half_z) + 0.5).astype(o_ref.dtype)


def _gates_proj(xb, w, bias, side_jobs=(), *, tm=1024, tn=COL_UNIT):
    n, d = xb.shape
    nc = w.shape[1]
    w_specs = _resident_split((d, nc))
    s_in, s_out, s_shapes = _side_specs(side_jobs, n // tm, lambda i: i)
    return pl.pallas_call(
        functools.partial(_gates_kernel, tn=tn, n_w=len(w_specs), side=_side_splits(side_jobs)),
        grid=(n // tm,),
        in_specs=[pl.BlockSpec((tm, d), lambda i: (i, 0)), *w_specs, _resident((1, nc)), *s_in],
        out_specs=[pl.BlockSpec((tm, nc), lambda i: (i, 0)), *s_out],
        out_shape=[jax.ShapeDtypeStruct((n, nc), BF16), *s_shapes],
        compiler_params=_params(1),
    )(xb, *([w] * len(w_specs)), bias.reshape(1, nc), *[w_ for w_, _ in side_jobs])


def _alibi_slopes():
    h = np.arange(1, N_Q_HEADS + 1, dtype=np.float32)
    return [float(s) * LOG2E for s in (2.0 ** (-8.0 * h / N_Q_HEADS)).astype(np.float32)]


def _split_halves(xp, odd, lo):
    xr = pltpu.roll(xp, HEAD_DIM, axis=1)
    zero = jnp.zeros_like(xp)
    if odd:
        return jnp.where(lo, xr, zero).astype(BF16), jnp.where(lo, zero, xp).astype(BF16)
    return jnp.where(lo, xp, zero).astype(BF16), jnp.where(lo, zero, xr).astype(BF16)


def _attn_kernel(sinks_ref, q_ref, kvp_ref, kvc_ref, *refs, tq, slopes, side):
    side_in, o_ref = refs[:len(side)], refs[len(side)]
    side_out, p_s = refs[len(side) + 1:-1], refs[-1]
    _run_side_casts(side, side_in, side_out)
    t = pl.program_id(1)
    half = GROUP // 2
    qi = lax.broadcasted_iota(jnp.int32, (WINDOW, 2 * WINDOW), 0)
    kj = lax.broadcasted_iota(jnp.int32, (WINDOW, 2 * WINDOW), 1)
    dist = WINDOW + qi - kj
    valid = (dist >= 0) & (dist < WINDOW)
    before_start = (kj < WINDOW) & (t == 0)
    distf = dist.astype(F32)
    lo = lax.broadcasted_iota(jnp.int32, (1, 2 * HEAD_DIM), 1) < HEAD_DIM
    ones_lo = jnp.where(lo, 1.0, 0.0).astype(BF16)
    ones_hi = jnp.where(lo, 0.0, 1.0).astype(BF16)
    kv = jnp.concatenate([kvp_ref[...], kvc_ref[...]], axis=0)
    nkv = kv.shape[0]
    dk = N_KV_HEADS * HEAD_DIM
    for h in range(N_KV_HEADS):
        pair, odd = divmod(h, 2)
        k_lo, k_hi = _split_halves(kv[:, 128 * pair:128 * pair + 128].astype(F32), odd, lo)
        v_lo, v_hi = _split_halves(kv[:, dk + 128 * pair:dk + 128 * pair + 128].astype(F32), odd, lo)
        v_lo = jnp.concatenate([v_lo, jnp.broadcast_to(ones_lo, (nkv, 2 * HEAD_DIM))], axis=1)
        v_hi = jnp.concatenate([v_hi, jnp.broadcast_to(ones_hi, (nkv, 2 * HEAD_DIM))], axis=1)
        bias = [[jnp.where(valid, slopes[GROUP * h + 2 * pp + par] * distf, jnp.inf) for par in range(2)]
                for pp in range(half)]
        for qb in range(tq // WINDOW):
            rs = slice(WINDOW * qb, WINDOW * qb + 2 * WINDOW)
            kk = jnp.concatenate([k_lo[rs], k_hi[rs]], axis=0)
            vv = jnp.concatenate([v_lo[rs], v_hi[rs]], axis=0)
            ql = jnp.concatenate(
                [q_ref[WINDOW * qb:WINDOW * (qb + 1), 512 * h + 128 * pp:512 * h + 128 * (pp + 1)]
                 for pp in range(half)], axis=0)
            s = lax.dot_general(ql, kk, (((1,), (1,)), ((), ())), preferred_element_type=F32)
            sink_terms = []
            for pp in range(half):
                terms = []
                for par in range(2):
                    sink = sinks_ref[GROUP * h + 2 * pp + par] * LOG2E
                    sp = s[WINDOW * pp:WINDOW * (pp + 1), 2 * WINDOW * par:2 * WINDOW * (par + 1)] - bias[pp][par]
                    if qb == 0:
                        sp = jnp.where(before_start, -jnp.inf, sp)
                    m = jnp.maximum(jnp.max(sp, axis=-1, keepdims=True), sink)
                    p_s[WINDOW * pp:WINDOW * (pp + 1), 2 * WINDOW * par:2 * WINDOW * (par + 1)] = (
                        jnp.exp2(sp - m).astype(BF16))
                    terms.append(jnp.exp2(sink - m))
                sink_terms.append(jnp.where(lo, terms[0], terms[1]))
            pv = jnp.dot(p_s[...], vv, preferred_element_type=F32)
            for pp in range(half):
                rows = slice(WINDOW * pp, WINDOW * (pp + 1))
                den = pv[rows, 2 * HEAD_DIM:] + sink_terms[pp]
                o_ref[WINDOW * qb:WINDOW * (qb + 1), 512 * h + 128 * pp:512 * h + 128 * (pp + 1)] = (
                    pv[rows, :2 * HEAD_DIM] / den).astype(o_ref.dtype)


def _attention(qkv, sinks, bsz, seq, side_jobs=(), *, tq=512):
    n = qkv.shape[0]
    d_attn = N_Q_HEADS * HEAD_DIM
    d_kv2 = 2 * N_KV_HEADS * HEAD_DIM
    nt = seq // tq
    per_w = tq // WINDOW
    kv_col = d_attn // d_kv2
    s_in, s_out, s_shapes = _side_specs(side_jobs, bsz * nt, lambda b, t: b * nt + t)
    return pl.pallas_call(
        functools.partial(_attn_kernel, tq=tq, slopes=_alibi_slopes(), side=_side_splits(side_jobs)),
        grid=(bsz, nt),
        in_specs=[
            pl.BlockSpec(memory_space=pltpu.SMEM),
            pl.BlockSpec((tq, d_attn), lambda b, t: (b * nt + t, 0)),
            pl.BlockSpec((WINDOW, d_kv2),
                         lambda b, t: (b * nt * per_w + jnp.maximum(t * per_w - 1, 0), kv_col)),
            pl.BlockSpec((tq, d_kv2), lambda b, t: (b * nt + t, kv_col)),
            *s_in,
        ],
        out_specs=[pl.BlockSpec((tq, d_attn), lambda b, t: (b * nt + t, 0)), *s_out],
        out_shape=[jax.ShapeDtypeStruct((n, d_attn), BF16), *s_shapes],
        scratch_shapes=[pltpu.VMEM((GROUP // 2 * WINDOW, 4 * WINDOW), BF16)],
        compiler_params=_params(2),
    )(sinks, qkv, qkv, qkv, *[w_ for w_, _ in side_jobs])


def _gate_matmul(xcb, wg_ref, sb):
    wide = wg_ref.shape[1]
    bw = wide // (N_RNN_BLOCKS // RNN_SUPER)
    outs = []
    for j0 in range(0, 2 * wide, MXU_TILE):
        j1 = min(j0 + MXU_TILE, 2 * wide)
        acc = None
        for i0 in range(0, wide, MXU_TILE):
            i1 = min(i0 + MXU_TILE, wide)
            nonzero = any(
                max(bw * b, i0) < min(bw * (b + 1), i1)
                and any(max(off + bw * b, j0) < min(off + bw * (b + 1), j1) for off in (0, wide))
                for b in range(wide // bw))
            if nonzero:
                d = jnp.dot(xcb[:, i0:i1], wg_ref[sb, i0:i1, j0:j1], preferred_element_type=F32)
                acc = d if acc is None else acc + d
        outs.append(acc)
    return jnp.concatenate(outs, axis=1)


def _rnn_kernel(x_ref, wrx_ref, wry_ref, cw_ref, cb_ref, wg_ref, ba_ref, bi_ref, lam_ref, z_ref,
                rx_s, gy_s, xc_s, pa_s, pu_s, hp_s, tail_s, carry_s, *, tt, tn):
    t = pl.program_id(1)
    c_all = z_ref.shape[1]
    csb = c_all // RNN_SUPER
    n_slabs = c_all // LANES
    per_chunk = tn // LANES
    ng = tt // SUBLANES
    width = cw_ref.shape[0]

    @pl.when(t == 0)
    def _():
        carry_s[...] = jnp.zeros_like(carry_s)
        tail_s[...] = jnp.zeros_like(tail_s)

    xb = x_ref[...].astype(BF16)
    for c in range(c_all // tn):
        acc = jnp.dot(xb, wrx_ref[:, c * tn:(c + 1) * tn], preferred_element_type=F32)
        for k in range(per_chunk):
            rx_s[c * per_chunk + k] = acc[:, LANES * k:LANES * (k + 1)]

    def ry_chunk(c):
        ry = jnp.dot(xb, wry_ref[:, c * tn:(c + 1) * tn], preferred_element_type=F32)
        gy = _gelu_tanh(ry)
        for k in range(per_chunk):
            gy_s[c * per_chunk + k] = gy[:, LANES * k:LANES * (k + 1)]

    ry_chunks = list(range(c_all // tn))

    first_row = lax.broadcasted_iota(jnp.int32, (ng, LANES), 0) == 0
    for s in range(n_slabs):
        ls = slice(LANES * s, LANES * (s + 1))
        x_ph = [rx_s[s, pl.ds(j, ng, stride=SUBLANES), :] for j in range(SUBLANES)]
        x_prev = {m: jnp.where(first_row, tail_s[s, m:m + 1, :], pltpu.roll(x_ph[m], 1, axis=0))
                  for m in range(SUBLANES - (width - 1), SUBLANES)}
        for j in range(SUBLANES):
            acc = cb_ref[:, ls]
            for k in range(width):
                src = j - (width - 1) + k
                acc = acc + cw_ref[k:k + 1, ls] * (x_ph[src] if src >= 0 else x_prev[src + SUBLANES])
            xc_s[ng * j:ng * (j + 1), ls] = acc
        tail_s[s] = rx_s[s, tt - SUBLANES:tt, :]

    for sb in range(RNN_SUPER):
        cs = slice(csb * sb, csb * (sb + 1))
        xc = xc_s[:, cs]
        pre = _gate_matmul(xc.astype(BF16), wg_ref, sb)
        n_now = -(-len(ry_chunks) // (RNN_SUPER - sb))
        for c in ry_chunks[:n_now]:
            ry_chunk(c)
        ry_chunks = ry_chunks[n_now:]
        ta = jnp.tanh(pre[:, :csb] + 0.5 * ba_ref[:, cs])
        ti = jnp.tanh(pre[:, csb:] + 0.5 * bi_ref[:, cs])
        half_c_sp = (0.5 * LRU_C) * jax.nn.softplus(-lam_ref[:, cs])
        neg_log_a = ta * half_c_sp + half_c_sp
        a = jnp.exp2(neg_log_a * (-LOG2E))
        hxc = 0.5 * xc
        ix = hxc * ti + hxc
        u = jnp.sqrt(jnp.tanh(neg_log_a) * (a * a + 1.0)) * ix
        pa = a[0:ng]
        pu = u[0:ng]
        pa_s[0:ng, cs] = pa
        pu_s[0:ng, cs] = pu
        for j in range(1, SUBLANES):
            rows = slice(ng * j, ng * (j + 1))
            pu = a[rows] * pu + u[rows]
            pa = a[rows] * pa
            pa_s[rows, cs] = pa
            pu_s[rows, cs] = pu

    last = ng * (SUBLANES - 1)

    def body(g, carry):
        hp_s[pl.ds(g, 1), :] = carry
        return pa_s[pl.ds(last + g, 1), :] * carry + pu_s[pl.ds(last + g, 1), :]

    carry_s[...] = lax.fori_loop(0, ng, body, carry_s[...], unroll=4)

    for s in range(n_slabs):
        ls = slice(LANES * s, LANES * (s + 1))
        hp = hp_s[:, ls]
        for j in range(SUBLANES):
            rows = slice(ng * j, ng * (j + 1))
            h = pa_s[rows, ls] * hp + pu_s[rows, ls]
            rx_s[s, pl.ds(j, ng, stride=SUBLANES), :] = h * gy_s[s, pl.ds(j, ng, stride=SUBLANES), :]
        z_ref[:, ls] = rx_s[s].astype(z_ref.dtype)


def _rnn(x, w_rxy, conv_w, conv_b, wg, ba, bi, lam, bsz, seq, *, tt=256, tn=1280):
    n, d = x.shape
    c = w_rxy.shape[1] // 2
    nt = seq // tt
    row = lambda b, t: (b * nt + t, 0)
    vec = lambda a: a.reshape(1, c)
    slab = lambda rows: pltpu.VMEM((c // LANES, rows, LANES), F32)
    return pl.pallas_call(
        functools.partial(_rnn_kernel, tt=tt, tn=tn),
        grid=(bsz, nt),
        in_specs=[
            pl.BlockSpec((tt, d), row),
            pl.BlockSpec((d, c), lambda b, t: (0, 0), pipeline_mode=pl.Buffered(1)),
            pl.BlockSpec((d, c), lambda b, t: (0, 1), pipeline_mode=pl.Buffered(1)),
            _resident(conv_w.shape), _resident((1, c)), _resident(wg.shape),
            _resident((1, c)), _resident((1, c)), _resident((1, c)),
        ],
        out_specs=pl.BlockSpec((tt, c), row),
        out_shape=jax.ShapeDtypeStruct((n, c), BF16),
        scratch_shapes=[
            slab(tt), slab(tt),
            pltpu.VMEM((tt, c), F32), pltpu.VMEM((tt, c), F32), pltpu.VMEM((tt, c), F32),
            pltpu.VMEM((tt // SUBLANES, c), F32),
            slab(SUBLANES),
            pltpu.VMEM((1, c), F32),
        ],
        compiler_params=_params(2),
    )(x, w_rxy, w_rxy, conv_w, vec(conv_b), wg, vec(ba), vec(bi), vec(lam))


def _superblock_kernel(wa_ref, wi_ref, o_ref):
    per, bw, _ = wa_ref.shape
    wide = per * bw
    row = lax.broadcasted_iota(jnp.int32, (bw, wide), 0)
    col = lax.broadcasted_iota(jnp.int32, (bw, wide), 1)
    for i in range(per):
        place = jnp.where(col == row + bw * i, 0.5, 0.0).astype(BF16)
        rs = slice(bw * i, bw * (i + 1))
        o_ref[0, rs, 0:wide] = jnp.dot(wa_ref[i].astype(BF16), place, preferred_element_type=F32).astype(o_ref.dtype)
        o_ref[0, rs, wide:2 * wide] = jnp.dot(wi_ref[i].astype(BF16), place,
                                              preferred_element_type=F32).astype(o_ref.dtype)


def _gate_superblocks(wa, wi):
    nb, bw, _ = wa.shape
    per = nb // RNN_SUPER
    blk = pl.BlockSpec((per, bw, bw), lambda s: (s, 0, 0))
    return pl.pallas_call(
        _superblock_kernel,
        grid=(RNN_SUPER,),
        in_specs=[blk, blk],
        out_specs=pl.BlockSpec((1, per * bw, 2 * per * bw), lambda s: (s, 0, 0)),
        out_shape=jax.ShapeDtypeStruct((RNN_SUPER, per * bw, 2 * per * bw), BF16),
        compiler_params=_params(1),
    )(wa, wi)


def _merge_kernel(o_ref, z_ref, g_ref, *refs, tn, n_split, side):
    wa_refs, wr_refs, refs = refs[:n_split], refs[n_split:2 * n_split], refs[2 * n_split:]
    side_in, m_ref, side_out = refs[:len(side)], refs[len(side)], refs[len(side) + 1:]
    _run_side_casts(side, side_in, side_out)
    o = o_ref[...]
    z = z_ref[...]
    d = m_ref.shape[1]
    for (wa_ref, lc, gc), (wr_ref, _, _) in zip(_col_chunks(wa_refs, tn), _col_chunks(wr_refs, tn)):
        ya = jnp.dot(o, wa_ref[:, lc], preferred_element_type=F32)
        yr = jnp.dot(z, wr_ref[:, lc], preferred_element_type=F32)
        ga = g_ref[:, gc].astype(F32)
        gr = g_ref[:, d + gc.start:d + gc.stop].astype(F32)
        m_ref[:, gc] = (ga * ya + gr * yr).astype(m_ref.dtype)


def _merge(o, z, gates, wa, wr, side_jobs=(), *, tm=512, tn=COL_UNIT):
    n, d = o.shape
    wa_specs, wr_specs = _resident_split(wa.shape), _resident_split(wr.shape)
    assert len(wa_specs) == len(wr_specs)
    s_in, s_out, s_shapes = _side_specs(side_jobs, n // tm, lambda i: i)
    return pl.pallas_call(
        functools.partial(_merge_kernel, tn=tn, n_split=len(wa_specs), side=_side_splits(side_jobs)),
        grid=(n // tm,),
        in_specs=[
            pl.BlockSpec((tm, d), lambda i: (i, 0)),
            pl.BlockSpec((tm, z.shape[1]), lambda i: (i, 0)),
            pl.BlockSpec((tm, 2 * d), lambda i: (i, 0)),
            *wa_specs, *wr_specs, *s_in,
        ],
        out_specs=[pl.BlockSpec((tm, d), lambda i: (i, 0)), *s_out],
        out_shape=[jax.ShapeDtypeStruct((n, d), BF16), *s_shapes],
        compiler_params=_params(1),
    )(o, z, gates, *([wa] * len(wa_specs)), *([wr] * len(wr_specs)), *[w_ for w_, _ in side_jobs])


def _out_ln_kernel(m_ref, x_ref, *refs, tn, alpha):
    *w_refs, g_ref, b_ref, y_ref = refs
    for r in range(m_ref.shape[0] // ROW_SUB):
        rs = slice(r * ROW_SUB, (r + 1) * ROW_SUB)
        m = m_ref[rs, :]
        for w_ref, lc, gc in _col_chunks(w_refs, tn):
            y_ref[rs, gc] = alpha * x_ref[rs, gc] + jnp.dot(m, w_ref[:, lc], preferred_element_type=F32)
        y_ref[rs, :] = _layer_norm(y_ref[rs, :], g_ref[...], b_ref[...])


def _out_ln(m, x, w, g, b, alpha, *, tm=1024, tn=COL_UNIT):
    n, d = x.shape
    w_specs = _resident_split(w.shape)
    return pl.pallas_call(
        functools.partial(_out_ln_kernel, tn=tn, alpha=alpha),
        grid=(n // tm,),
        in_specs=[
            pl.BlockSpec((tm, m.shape[1]), lambda i: (i, 0)),
            pl.BlockSpec((tm, d), lambda i: (i, 0)),
            *w_specs, _resident((1, d)), _resident((1, d)),
        ],
        out_specs=pl.BlockSpec((tm, d), lambda i: (i, 0)),
        out_shape=jax.ShapeDtypeStruct((n, d), F32),
        compiler_params=_params(1),
    )(m, x, *([w] * len(w_specs)), g.reshape(1, d), b.reshape(1, d))


def _ffn_kernel(x_ref, wu_ref, wg_ref, cw_ref, cb_ref, wd_ref, g_ref, b_ref, y_ref,
                xb_s, gs_s, tail_s, *, tm, alpha, blocks_per_seq):
    i = pl.program_id(0)
    c = pl.program_id(1)
    first = (i % blocks_per_seq) == 0

    @pl.when(c == 0)
    def _():
        xb_s[...] = x_ref[...].astype(BF16)
        y_ref[...] = alpha * x_ref[...]

    @pl.when(first)
    def _():
        gs_s[0:HALO] = jnp.zeros((HALO, gs_s.shape[1]), F32)

    @pl.when(jnp.logical_not(first))
    def _():
        gs_s[0:HALO] = tail_s[c]

    xb = xb_s[...]
    gs_s[HALO:HALO + tm] = jnp.dot(xb, wg_ref[...], preferred_element_type=F32)
    tail_s[c] = gs_s[tm:tm + HALO]
    up = jnp.dot(xb, wu_ref[...], preferred_element_type=F32)
    width = cw_ref.shape[0]
    gate = cb_ref[...]
    for k in range(width):
        off = HALO - (width - 1) + k
        gate = gate + cw_ref[k:k + 1, :] * gs_s[off:off + tm, :]
    hid = (jax.nn.gelu(gate, approximate=True) * up).astype(BF16)
    y_ref[...] += jnp.dot(hid, wd_ref[...], preferred_element_type=F32)

    @pl.when(c == pl.num_programs(1) - 1)
    def _():
        y_ref[...] = _layer_norm(y_ref[...], g_ref[...], b_ref[...])


def _ffn(x, wu, wg, conv_w, conv_b, wd, g, b, alpha, seq, *, tm=512, tf=1024):
    n, d = x.shape
    dff = wu.shape[1]
    return pl.pallas_call(
        functools.partial(_ffn_kernel, tm=tm, alpha=alpha, blocks_per_seq=seq // tm),
        grid=(n // tm, dff // tf),
        in_specs=[
            pl.BlockSpec((tm, d), lambda i, c: (i, 0)),
            pl.BlockSpec((d, tf), lambda i, c: (0, c)),
            pl.BlockSpec((d, tf), lambda i, c: (0, c)),
            pl.BlockSpec((conv_w.shape[0], tf), lambda i, c: (0, c)),
            pl.BlockSpec((1, tf), lambda i, c: (0, c)),
            pl.BlockSpec((tf, d), lambda i, c: (c, 0)),
            _resident((1, d)), _resident((1, d)),
        ],
        out_specs=pl.BlockSpec((tm, d), lambda i, c: (i, 0)),
        out_shape=jax.ShapeDtypeStruct((n, d), F32),
        scratch_shapes=[pltpu.VMEM((tm, d), BF16), pltpu.VMEM((HALO + tm, tf), F32),
                        pltpu.VMEM((dff // tf, HALO, tf), F32)],
        compiler_params=_params(2),
    )(x, wu, wg, conv_w, conv_b.reshape(1, dff), wd, g.reshape(1, d), b.reshape(1, d))


def kernel(x, w_in, b_gate, rnn_conv_w, rnn_conv_b, lru_wa, lru_ba, lru_wi, lru_bi, lru_lambda, attn_sinks,
           w_attn_proj, w_rnn_proj, w_out, ln1_g, ln1_b, ffn_w_up, ffn_w_gate, ffn_conv_w, ffn_conv_b,
           ffn_w_down, ln2_g, ln2_b):
    bsz, seq, d = x.shape
    depth = w_in.shape[0]
    alpha = float((2 * depth) ** 0.25)
    d_attn = N_Q_HEADS * HEAD_DIM
    d_qkv = d_attn + 2 * N_KV_HEADS * HEAD_DIM
    d_rnn = rnn_conv_w.shape[-1]

    h = x.reshape(bsz * seq, d)
    for l in range(depth):
        d_in = w_in.shape[-1]
        rest_of_w_in = (w_in[l], ((d_qkv, d_qkv + 2 * d_rnn, 1.0), (d_qkv + 2 * d_rnn, d_in, 0.5)))
        qkv, hb, w_rxy, w_gl_half = _qkv_proj(h, w_in[l], d_qkv, d_attn, HEAD_DIM ** -0.5 * LOG2E, (rest_of_w_in,))
        gates, w_up = _gates_proj(hb, w_gl_half, b_gate[l], (_whole(ffn_w_up[l]),))
        o, w_gate, w_ap, w_rp = _attention(qkv, attn_sinks[l], bsz, seq,
                                           (_whole(ffn_w_gate[l]), _whole(w_attn_proj[l]), _whole(w_rnn_proj[l])))

        wg = _gate_superblocks(lru_wa[l], lru_wi[l])
        z = _rnn(h, w_rxy, rnn_conv_w[l], rnn_conv_b[l], wg, lru_ba[l], lru_bi[l], lru_lambda[l], bsz, seq)

        m, w_down, w_o = _merge(o, z, gates, w_ap, w_rp, (_whole(ffn_w_down[l]), _whole(w_out[l])))
        h = _out_ln(m, h, w_o, ln1_g[l], ln1_b[l], alpha)
        h = _ffn(h, w_up, w_gate, ffn_conv_w[l], ffn_conv_b[l], w_down, ln2_g[l], ln2_b[l], alpha, seq)
    return h.reshape(bsz, seq, d)
```

```python
import functools

import numpy as np
import jax
import jax.numpy as jnp
from jax import lax
from jax.experimental import pallas as pl
from jax.experimental.pallas import tpu as pltpu

F32 = jnp.float32
BF16 = jnp.bfloat16

HEAD_DIM = 64
N_Q_HEADS = 32
N_KV_HEADS = 4
GROUP = N_Q_HEADS // N_KV_HEADS
WINDOW = 128
N_RNN_BLOCKS = 16
RNN_SUPER = 4
LRU_C = 8.0
LN_EPS = 1e-5
LOG2E = 1.4426950408889634
GELU_C0 = 0.7978845608028654
GELU_C1 = GELU_C0 * 0.044715
HALO = 16
SUBLANES = 8
LANES = 128
COL_UNIT = 512
MXU_TILE = 256
ROW_SUB = 512
BF16_ROWS = 16
VMEM_LIMIT = 56 * 1024 * 1024


def _params(n_axes, vmem_limit=VMEM_LIMIT):
    return pltpu.CompilerParams(dimension_semantics=("arbitrary",) * n_axes,
                                vmem_limit_bytes=vmem_limit)


def _resident(shape):
    nd = len(shape)
    return pl.BlockSpec(shape, lambda *_: (0,) * nd, pipeline_mode=pl.Buffered(1))


def _split_cols(width):
    if width % (2 * COL_UNIT):
        return [(width, 0)]
    return [(width - COL_UNIT, 0), (COL_UNIT, width - COL_UNIT)]


def _resident_split(shape):
    rows, width = shape
    return [pl.BlockSpec((rows, w), functools.partial(lambda j, *_: (0, j), off // w), pipeline_mode=pl.Buffered(1))
            for w, off in _split_cols(width)]


def _col_chunks(refs, tn):
    out, off = [], 0
    for r in refs:
        for c in range(r.shape[1] // tn):
            out.append((r, slice(c * tn, (c + 1) * tn), slice(off, off + tn)))
            off += tn
    return out


def _gelu_tanh(x):
    hx = 0.5 * x
    return hx * jnp.tanh(x * (GELU_C0 + GELU_C1 * (x * x))) + hx


def _layer_norm(v, g, b):
    mu = jnp.mean(v, axis=-1, keepdims=True)
    d = v - mu
    var = jnp.mean(d * d, axis=-1, keepdims=True)
    return d * lax.rsqrt(var + LN_EPS) * g + b


def _side_specs(jobs, steps, row_index):
    in_specs, out_specs, out_shapes = [], [], []
    for w, splits in jobs:
        rows, cols = w.shape
        tr = rows // steps
        assert tr * steps == rows and tr % BF16_ROWS == 0
        in_specs.append(pl.BlockSpec((tr, cols), lambda *g: (row_index(*g), 0)))
        for lo, hi, _ in splits:
            out_specs.append(pl.BlockSpec((tr, hi - lo), lambda *g: (row_index(*g), 0)))
            out_shapes.append(jax.ShapeDtypeStruct((rows, hi - lo), BF16))
    return in_specs, out_specs, out_shapes


def _side_splits(jobs):
    return tuple(splits for _, splits in jobs)


def _run_side_casts(side, in_refs, out_refs):
    out_refs = list(out_refs)
    for splits, w_ref in zip(side, in_refs):
        for lo, hi, scale in splits:
            w = w_ref[:, lo:hi]
            out_refs.pop(0)[...] = (w if scale == 1.0 else w * scale).astype(BF16)


def _whole(w):
    return (w, ((0, w.shape[1], 1.0),))


def _qkv_kernel(x_ref, w_ref, *refs, tn, n_scaled, scale, side):
    side_in, (o_ref, xb_ref), side_out = refs[:len(side)], refs[len(side):len(side) + 2], refs[len(side) + 2:]
    _run_side_casts(side, side_in, side_out)
    for r in range(x_ref.shape[0] // ROW_SUB):
        rs = slice(r * ROW_SUB, (r + 1) * ROW_SUB)
        xb = x_ref[rs, :].astype(BF16)
        xb_ref[rs, :] = xb
        for c in range(w_ref.shape[1] // tn):
            cs = slice(c * tn, (c + 1) * tn)
            acc = jnp.dot(xb, w_ref[:, cs].astype(BF16), preferred_element_type=F32)
            if c < n_scaled:
                acc = acc * scale
            o_ref[rs, cs] = acc.astype(o_ref.dtype)


def _qkv_proj(x, w, nc, d_attn, scale, side_jobs=(), *, tm=512, tn=512):
    n, d = x.shape
    s_in, s_out, s_shapes = _side_specs(side_jobs, n // tm, lambda i: i)
    return pl.pallas_call(
        functools.partial(_qkv_kernel, tn=tn, n_scaled=d_attn // tn, scale=scale, side=_side_splits(side_jobs)),
        grid=(n // tm,),
        in_specs=[pl.BlockSpec((tm, d), lambda i: (i, 0)), _resident((d, nc)), *s_in],
        out_specs=[pl.BlockSpec((tm, nc), lambda i: (i, 0)), pl.BlockSpec((tm, d), lambda i: (i, 0)), *s_out],
        out_shape=[jax.ShapeDtypeStruct((n, nc), BF16), jax.ShapeDtypeStruct((n, d), BF16), *s_shapes],
        compiler_params=_params(1),
    )(x, w, *[w_ for w_, _ in side_jobs])


def _gates_kernel(x_ref, *refs, tn, n_w, side):
    w_refs, b_ref = refs[:n_w], refs[n_w]
    side_in, o_ref, side_out = refs[n_w + 1:n_w + 1 + len(side)], refs[n_w + 1 + len(side)], refs[n_w + 2 + len(side):]
    _run_side_casts(side, side_in, side_out)
    for r in range(x_ref.shape[0] // ROW_SUB):
        rs = slice(r * ROW_SUB, (r + 1) * ROW_SUB)
        xb = x_ref[rs, :]
        for w_ref, lc, gc in _col_chunks(w_refs, tn):
            half_z = jnp.dot(xb, w_ref[:, lc], preferred_element_type=F32) + 0.5 * b_ref[:, gc]
            o_ref[rs, gc] = (0.5 * jnp.tanh(half_z) + 0.5).astype(o_ref.dtype)


def _gates_proj(xb, w, bias, side_jobs=(), *, tm=1024, tn=COL_UNIT):
    n, d = xb.shape
    nc = w.shape[1]
    w_specs = _resident_split((d, nc))
    s_in, s_out, s_shapes = _side_specs(side_jobs, n // tm, lambda i: i)
    return pl.pallas_call(
        functools.partial(_gates_kernel, tn=tn, n_w=len(w_specs), side=_side_splits(side_jobs)),
        grid=(n // tm,),
        in_specs=[pl.BlockSpec((tm, d), lambda i: (i, 0)), *w_specs, _resident((1, nc)), *s_in],
        out_specs=[pl.BlockSpec((tm, nc), lambda i: (i, 0)), *s_out],
        out_shape=[jax.ShapeDtypeStruct((n, nc), BF16), *s_shapes],
        compiler_params=_params(1),
    )(xb, *([w] * len(w_specs)), bias.reshape(1, nc), *[w_ for w_, _ in side_jobs])


def _alibi_slopes():
    h = np.arange(1, N_Q_HEADS + 1, dtype=np.float32)
    return [float(s) * LOG2E for s in (2.0 ** (-8.0 * h / N_Q_HEADS)).astype(np.float32)]


def _split_halves(xp, odd, lo):
    xr = pltpu.roll(xp, HEAD_DIM, axis=1)
    zero = jnp.zeros_like(xp)
    if odd:
        return jnp.where(lo, xr, zero).astype(BF16), jnp.where(lo, zero, xp).astype(BF16)
    return jnp.where(lo, xp, zero).astype(BF16), jnp.where(lo, zero, xr).astype(BF16)


def _attn_kernel(sinks_ref, q_ref, kvp_ref, kvc_ref, *refs, tq, slopes, side):
    side_in, o_ref = refs[:len(side)], refs[len(side)]
    side_out, p_s = refs[len(side) + 1:-1], refs[-1]
    _run_side_casts(side, side_in, side_out)
    t = pl.program_id(1)
    half = GROUP // 2
    qi = lax.broadcasted_iota(jnp.int32, (WINDOW, 2 * WINDOW), 0)
    kj = lax.broadcasted_iota(jnp.int32, (WINDOW, 2 * WINDOW), 1)
    dist = WINDOW + qi - kj
    valid = (dist >= 0) & (dist < WINDOW)
    before_start = (kj < WINDOW) & (t == 0)
    distf = dist.astype(F32)
    lo = lax.broadcasted_iota(jnp.int32, (1, 2 * HEAD_DIM), 1) < HEAD_DIM
    ones_lo = jnp.where(lo, 1.0, 0.0).astype(BF16)
    ones_hi = jnp.where(lo, 0.0, 1.0).astype(BF16)
    kv = jnp.concatenate([kvp_ref[...], kvc_ref[...]], axis=0)
    nkv = kv.shape[0]
    dk = N_KV_HEADS * HEAD_DIM
    for h in range(N_KV_HEADS):
        pair, odd = divmod(h, 2)
        k_lo, k_hi = _split_halves(kv[:, 128 * pair:128 * pair + 128].astype(F32), odd, lo)
        v_lo, v_hi = _split_halves(kv[:, dk + 128 * pair:dk + 128 * pair + 128].astype(F32), odd, lo)
        v_lo = jnp.concatenate([v_lo, jnp.broadcast_to(ones_lo, (nkv, 2 * HEAD_DIM))], axis=1)
        v_hi = jnp.concatenate([v_hi, jnp.broadcast_to(ones_hi, (nkv, 2 * HEAD_DIM))], axis=1)
        bias = [[jnp.where(valid, slopes[GROUP * h + 2 * pp + par] * distf, jnp.inf) for par in range(2)]
                for pp in range(half)]
        for qb in range(tq // WINDOW):
            rs = slice(WINDOW * qb, WINDOW * qb + 2 * WINDOW)
            kk = jnp.concatenate([k_lo[rs], k_hi[rs]], axis=0)
            vv = jnp.concatenate([v_lo[rs], v_hi[rs]], axis=0)
            ql = jnp.concatenate(
                [q_ref[WINDOW * qb:WINDOW * (qb + 1), 512 * h + 128 * pp:512 * h + 128 * (pp + 1)]
                 for pp in range(half)], axis=0)
            s = lax.dot_general(ql, kk, (((1,), (1,)), ((), ())), preferred_element_type=F32)
            sink_terms = []
            for pp in range(half):
                terms = []
                for par in range(2):
                    sink = sinks_ref[GROUP * h + 2 * pp + par] * LOG2E
                    sp = s[WINDOW * pp:WINDOW * (pp + 1), 2 * WINDOW * par:2 * WINDOW * (par + 1)] - bias[pp][par]
                    if qb == 0:
                        sp = jnp.where(before_start, -jnp.inf, sp)
                    m = jnp.maximum(jnp.max(sp, axis=-1, keepdims=True), sink)
                    p_s[WINDOW * pp:WINDOW * (pp + 1), 2 * WINDOW * par:2 * WINDOW * (par + 1)] = (
                        jnp.exp2(sp - m).astype(BF16))
                    terms.append(jnp.exp2(sink - m))
                sink_terms.append(jnp.where(lo, terms[0], terms[1]))
            pv = jnp.dot(p_s[...], vv, preferred_element_type=F32)
            for pp in range(half):
                rows = slice(WINDOW * pp, WINDOW * (pp + 1))
                den = pv[rows, 2 * HEAD_DIM:] + sink_terms[pp]
                o_ref[WINDOW * qb:WINDOW * (qb + 1), 512 * h + 128 * pp:512 * h + 128 * (pp + 1)] = (
                    pv[rows, :2 * HEAD_DIM] / den).astype(o_ref.dtype)


def _attention(qkv, sinks, bsz, seq, side_jobs=(), *, tq=1024):
    n = qkv.shape[0]
    d_attn = N_Q_HEADS * HEAD_DIM
    d_kv2 = 2 * N_KV_HEADS * HEAD_DIM
    nt = seq // tq
    per_w = tq // WINDOW
    kv_col = d_attn // d_kv2
    s_in, s_out, s_shapes = _side_specs(side_jobs, bsz * nt, lambda b, t: b * nt + t)
    return pl.pallas_call(
        functools.partial(_attn_kernel, tq=tq, slopes=_alibi_slopes(), side=_side_splits(side_jobs)),
        grid=(bsz, nt),
        in_specs=[
            pl.BlockSpec(memory_space=pltpu.SMEM),
            pl.BlockSpec((tq, d_attn), lambda b, t: (b * nt + t, 0)),
            pl.BlockSpec((WINDOW, d_kv2),
                         lambda b, t: (b * nt * per_w + jnp.maximum(t * per_w - 1, 0), kv_col)),
            pl.BlockSpec((tq, d_kv2), lambda b, t: (b * nt + t, kv_col)),
            *s_in,
        ],
        out_specs=[pl.BlockSpec((tq, d_attn), lambda b, t: (b * nt + t, 0)), *s_out],
        out_shape=[jax.ShapeDtypeStruct((n, d_attn), BF16), *s_shapes],
        scratch_shapes=[pltpu.VMEM((GROUP // 2 * WINDOW, 4 * WINDOW), BF16)],
        compiler_params=_params(2),
    )(sinks, qkv, qkv, qkv, *[w_ for w_, _ in side_jobs])


def _gate_matmul(xcb, wg_ref, sb):
    wide = wg_ref.shape[1]
    bw = wide // (N_RNN_BLOCKS // RNN_SUPER)
    outs = []
    for j0 in range(0, 2 * wide, MXU_TILE):
        j1 = min(j0 + MXU_TILE, 2 * wide)
        acc = None
        for i0 in range(0, wide, MXU_TILE):
            i1 = min(i0 + MXU_TILE, wide)
            nonzero = any(
                max(bw * b, i0) < min(bw * (b + 1), i1)
                and any(max(off + bw * b, j0) < min(off + bw * (b + 1), j1) for off in (0, wide))
                for b in range(wide // bw))
            if nonzero:
                d = jnp.dot(xcb[:, i0:i1], wg_ref[sb, i0:i1, j0:j1], preferred_element_type=F32)
                acc = d if acc is None else acc + d
        outs.append(acc)
    return jnp.concatenate(outs, axis=1)


def _rnn_kernel(x_ref, wrx_ref, wry_ref, cw_ref, cb_ref, wg_ref, ba_ref, bi_ref, lam_ref, z_ref,
                rx_s, gy_s, xc_s, pa_s, pu_s, hp_s, tail_s, carry_s, *, tt, tn):
    t = pl.program_id(1)
    c_all = z_ref.shape[1]
    csb = c_all // RNN_SUPER
    n_slabs = c_all // LANES
    per_chunk = tn // LANES
    ng = tt // SUBLANES
    width = cw_ref.shape[0]

    @pl.when(t == 0)
    def _():
        carry_s[...] = jnp.zeros_like(carry_s)
        tail_s[...] = jnp.zeros_like(tail_s)

    xb = x_ref[...].astype(BF16)
    for c in range(c_all // tn):
        acc = jnp.dot(xb, wrx_ref[:, c * tn:(c + 1) * tn], preferred_element_type=F32)
        for k in range(per_chunk):
            rx_s[c * per_chunk + k] = acc[:, LANES * k:LANES * (k + 1)]

    def ry_chunk(c):
        ry = jnp.dot(xb, wry_ref[:, c * tn:(c + 1) * tn], preferred_element_type=F32)
        gy = _gelu_tanh(ry)
        for k in range(per_chunk):
            gy_s[c * per_chunk + k] = gy[:, LANES * k:LANES * (k + 1)]

    ry_chunks = list(range(c_all // tn))

    first_row = lax.broadcasted_iota(jnp.int32, (ng, LANES), 0) == 0
    for s in range(n_slabs):
        ls = slice(LANES * s, LANES * (s + 1))
        x_ph = [rx_s[s, pl.ds(j, ng, stride=SUBLANES), :] for j in range(SUBLANES)]
        x_prev = {m: jnp.where(first_row, tail_s[s, m:m + 1, :], pltpu.roll(x_ph[m], 1, axis=0))
                  for m in range(SUBLANES - (width - 1), SUBLANES)}
        for j in range(SUBLANES):
            acc = cb_ref[:, ls]
            for k in range(width):
                src = j - (width - 1) + k
                acc = acc + cw_ref[k:k + 1, ls] * (x_ph[src] if src >= 0 else x_prev[src + SUBLANES])
            xc_s[ng * j:ng * (j + 1), ls] = acc
        tail_s[s] = rx_s[s, tt - SUBLANES:tt, :]

    for sb in range(RNN_SUPER):
        cs = slice(csb * sb, csb * (sb + 1))
        xc = xc_s[:, cs]
        pre = _gate_matmul(xc.astype(BF16), wg_ref, sb)
        n_now = -(-len(ry_chunks) // (RNN_SUPER - sb))
        for c in ry_chunks[:n_now]:
            ry_chunk(c)
        ry_chunks = ry_chunks[n_now:]
        ta = jnp.tanh(pre[:, :csb] + 0.5 * ba_ref[:, cs])
        ti = jnp.tanh(pre[:, csb:] + 0.5 * bi_ref[:, cs])
        half_c_sp = (0.5 * LRU_C) * jax.nn.softplus(-lam_ref[:, cs])
        neg_log_a = ta * half_c_sp + half_c_sp
        a = jnp.exp2(neg_log_a * (-LOG2E))
        hxc = 0.5 * xc
        ix = hxc * ti + hxc
        u = jnp.sqrt(jnp.tanh(neg_log_a) * (a * a + 1.0)) * ix
        pa = a[0:ng]
        pu = u[0:ng]
        pa_s[0:ng, cs] = pa
        pu_s[0:ng, cs] = pu
        for j in range(1, SUBLANES):
            rows = slice(ng * j, ng * (j + 1))
            pu = a[rows] * pu + u[rows]
            pa = a[rows] * pa
            pa_s[rows, cs] = pa
            pu_s[rows, cs] = pu

    last = ng * (SUBLANES - 1)

    def body(g, carry):
        hp_s[pl.ds(g, 1), :] = carry
        return pa_s[pl.ds(last + g, 1), :] * carry + pu_s[pl.ds(last + g, 1), :]

    carry_s[...] = lax.fori_loop(0, ng, body, carry_s[...], unroll=4)

    for s in range(n_slabs):
        ls = slice(LANES * s, LANES * (s + 1))
        hp = hp_s[:, ls]
        for j in range(SUBLANES):
            rows = slice(ng * j, ng * (j + 1))
            h = pa_s[rows, ls] * hp + pu_s[rows, ls]
            rx_s[s, pl.ds(j, ng, stride=SUBLANES), :] = h * gy_s[s, pl.ds(j, ng, stride=SUBLANES), :]
        z_ref[:, ls] = rx_s[s].astype(z_ref.dtype)


def _rnn(x, w_rxy, conv_w, conv_b, wg, ba, bi, lam, bsz, seq, *, tt=256, tn=1280):
    n, d = x.shape
    c = w_rxy.shape[1] // 2
    nt = seq // tt
    row = lambda b, t: (b * nt + t, 0)
    vec = lambda a: a.reshape(1, c)
    slab = lambda rows: pltpu.VMEM((c // LANES, rows, LANES), F32)
    return pl.pallas_call(
        functools.partial(_rnn_kernel, tt=tt, tn=tn),
        grid=(bsz, nt),
        in_specs=[
            pl.BlockSpec((tt, d), row),
            pl.BlockSpec((d, c), lambda b, t: (0, 0), pipeline_mode=pl.Buffered(1)),
            pl.BlockSpec((d, c), lambda b, t: (0, 1), pipeline_mode=pl.Buffered(1)),
            _resident(conv_w.shape), _resident((1, c)), _resident(wg.shape),
            _resident((1, c)), _resident((1, c)), _resident((1, c)),
        ],
        out_specs=pl.BlockSpec((tt, c), row),
        out_shape=jax.ShapeDtypeStruct((n, c), BF16),
        scratch_shapes=[
            slab(tt), slab(tt),
            pltpu.VMEM((tt, c), F32), pltpu.VMEM((tt, c), F32), pltpu.VMEM((tt, c), F32),
            pltpu.VMEM((tt // SUBLANES, c), F32),
            slab(SUBLANES),
            pltpu.VMEM((1, c), F32),
        ],
        compiler_params=_params(2),
    )(x, w_rxy, w_rxy, conv_w, vec(conv_b), wg, vec(ba), vec(bi), vec(lam))


def _superblock_kernel(wa_ref, wi_ref, o_ref):
    per, bw, _ = wa_ref.shape
    wide = per * bw
    row = lax.broadcasted_iota(jnp.int32, (bw, wide), 0)
    col = lax.broadcasted_iota(jnp.int32, (bw, wide), 1)
    for i in range(per):
        place = jnp.where(col == row + bw * i, 0.5, 0.0).astype(BF16)
        rs = slice(bw * i, bw * (i + 1))
        o_ref[0, rs, 0:wide] = jnp.dot(wa_ref[i].astype(BF16), place, preferred_element_type=F32).astype(o_ref.dtype)
        o_ref[0, rs, wide:2 * wide] = jnp.dot(wi_ref[i].astype(BF16), place,
                                              preferred_element_type=F32).astype(o_ref.dtype)


def _gate_superblocks(wa, wi):
    nb, bw, _ = wa.shape
    per = nb // RNN_SUPER
    blk = pl.BlockSpec((per, bw, bw), lambda s: (s, 0, 0))
    return pl.pallas_call(
        _superblock_kernel,
        grid=(RNN_SUPER,),
        in_specs=[blk, blk],
        out_specs=pl.BlockSpec((1, per * bw, 2 * per * bw), lambda s: (s, 0, 0)),
        out_shape=jax.ShapeDtypeStruct((RNN_SUPER, per * bw, 2 * per * bw), BF16),
        compiler_params=_params(1),
    )(wa, wi)


def _merge_kernel(o_ref, z_ref, g_ref, *refs, tn, n_split, side):
    wa_refs, wr_refs, refs = refs[:n_split], refs[n_split:2 * n_split], refs[2 * n_split:]
    side_in, m_ref, side_out = refs[:len(side)], refs[len(side)], refs[len(side) + 1:]
    _run_side_casts(side, side_in, side_out)
    o = o_ref[...]
    z = z_ref[...]
    d = m_ref.shape[1]
    for (wa_ref, lc, gc), (wr_ref, _, _) in zip(_col_chunks(wa_refs, tn), _col_chunks(wr_refs, tn)):
        ya = jnp.dot(o, wa_ref[:, lc], preferred_element_type=F32)
        yr = jnp.dot(z, wr_ref[:, lc], preferred_element_type=F32)
        ga = g_ref[:, gc].astype(F32)
        gr = g_ref[:, d + gc.start:d + gc.stop].astype(F32)
        m_ref[:, gc] = (ga * ya + gr * yr).astype(m_ref.dtype)


def _merge(o, z, gates, wa, wr, side_jobs=(), *, tm=512, tn=COL_UNIT):
    n, d = o.shape
    wa_specs, wr_specs = _resident_split(wa.shape), _resident_split(wr.shape)
    assert len(wa_specs) == len(wr_specs)
    s_in, s_out, s_shapes = _side_specs(side_jobs, n // tm, lambda i: i)
    return pl.pallas_call(
        functools.partial(_merge_kernel, tn=tn, n_split=len(wa_specs), side=_side_splits(side_jobs)),
        grid=(n // tm,),
        in_specs=[
            pl.BlockSpec((tm, d), lambda i: (i, 0)),
            pl.BlockSpec((tm, z.shape[1]), lambda i: (i, 0)),
            pl.BlockSpec((tm, 2 * d), lambda i: (i, 0)),
            *wa_specs, *wr_specs, *s_in,
        ],
        out_specs=[pl.BlockSpec((tm, d), lambda i: (i, 0)), *s_out],
        out_shape=[jax.ShapeDtypeStruct((n, d), BF16), *s_shapes],
        compiler_params=_params(1),
    )(o, z, gates, *([wa] * len(wa_specs)), *([wr] * len(wr_specs)), *[w_ for w_, _ in side_jobs])


def _out_ln_kernel(m_ref, x_ref, *refs, tn, alpha):
    *w_refs, g_ref, b_ref, y_ref = refs
    for r in range(m_ref.shape[0] // ROW_SUB):
        rs = slice(r * ROW_SUB, (r + 1) * ROW_SUB)
        m = m_ref[rs, :]
        for w_ref, lc, gc in _col_chunks(w_refs, tn):
            y_ref[rs, gc] = alpha * x_ref[rs, gc] + jnp.dot(m, w_ref[:, lc], preferred_element_type=F32)
        y_ref[rs, :] = _layer_norm(y_ref[rs, :], g_ref[...], b_ref[...])


def _out_ln(m, x, w, g, b, alpha, *, tm=1024, tn=COL_UNIT):
    n, d = x.shape
    w_specs = _resident_split(w.shape)
    return pl.pallas_call(
        functools.partial(_out_ln_kernel, tn=tn, alpha=alpha),
        grid=(n // tm,),
        in_specs=[
            pl.BlockSpec((tm, m.shape[1]), lambda i: (i, 0)),
            pl.BlockSpec((tm, d), lambda i: (i, 0)),
            *w_specs, _resident((1, d)), _resident((1, d)),
        ],
        out_specs=pl.BlockSpec((tm, d), lambda i: (i, 0)),
        out_shape=jax.ShapeDtypeStruct((n, d), F32),
        compiler_params=_params(1),
    )(m, x, *([w] * len(w_specs)), g.reshape(1, d), b.reshape(1, d))


def _ffn_kernel(x_ref, wu_ref, wg_ref, cw_ref, cb_ref, wd_ref, g_ref, b_ref, y_ref,
                xb_s, gs_s, tail_s, *, tm, alpha, blocks_per_seq):
    i = pl.program_id(0)
    c = pl.program_id(1)
    first = (i % blocks_per_seq) == 0

    @pl.when(c == 0)
    def _():
        xb_s[...] = x_ref[...].astype(BF16)
        y_ref[...] = alpha * x_ref[...]

    @pl.when(first)
    def _():
        gs_s[0:HALO] = jnp.zeros((HALO, gs_s.shape[1]), F32)

    @pl.when(jnp.logical_not(first))
    def _():
        gs_s[0:HALO] = tail_s[c]

    xb = xb_s[...]
    gs_s[HALO:HALO + tm] = jnp.dot(xb, wg_ref[...], preferred_element_type=F32)
    tail_s[c] = gs_s[tm:tm + HALO]
    up = jnp.dot(xb, wu_ref[...], preferred_element_type=F32)
    width = cw_ref.shape[0]
    gate = cb_ref[...]
    for k in range(width):
        off = HALO - (width - 1) + k
        gate = gate + cw_ref[k:k + 1, :] * gs_s[off:off + tm, :]
    hid = (jax.nn.gelu(gate, approximate=True) * up).astype(BF16)
    y_ref[...] += jnp.dot(hid, wd_ref[...], preferred_element_type=F32)

    @pl.when(c == pl.num_programs(1) - 1)
    def _():
        y_ref[...] = _layer_norm(y_ref[...], g_ref[...], b_ref[...])


def _ffn(x, wu, wg, conv_w, conv_b, wd, g, b, alpha, seq, *, tm=512, tf=1024):
    n, d = x.shape
    dff = wu.shape[1]
    return pl.pallas_call(
        functools.partial(_ffn_kernel, tm=tm, alpha=alpha, blocks_per_seq=seq // tm),
        grid=(n // tm, dff // tf),
        in_specs=[
            pl.BlockSpec((tm, d), lambda i, c: (i, 0)),
            pl.BlockSpec((d, tf), lambda i, c: (0, c)),
            pl.BlockSpec((d, tf), lambda i, c: (0, c)),
            pl.BlockSpec((conv_w.shape[0], tf), lambda i, c: (0, c)),
            pl.BlockSpec((1, tf), lambda i, c: (0, c)),
            pl.BlockSpec((tf, d), lambda i, c: (c, 0)),
            _resident((1, d)), _resident((1, d)),
        ],
        out_specs=pl.BlockSpec((tm, d), lambda i, c: (i, 0)),
        out_shape=jax.ShapeDtypeStruct((n, d), F32),
        scratch_shapes=[pltpu.VMEM((tm, d), BF16), pltpu.VMEM((HALO + tm, tf), F32),
                        pltpu.VMEM((dff // tf, HALO, tf), F32)],
        compiler_params=_params(2),
    )(x, wu, wg, conv_w, conv_b.reshape(1, dff), wd, g.reshape(1, d), b.reshape(1, d))


def kernel(x, w_in, b_gate, rnn_conv_w, rnn_conv_b, lru_wa, lru_ba, lru_wi, lru_bi, lru_lambda, attn_sinks,
           w_attn_proj, w_rnn_proj, w_out, ln1_g, ln1_b, ffn_w_up, ffn_w_gate, ffn_conv_w, ffn_conv_b,
           ffn_w_down, ln2_g, ln2_b):
    bsz, seq, d = x.shape
    depth = w_in.shape[0]
    alpha = float((2 * depth) ** 0.25)
    d_attn = N_Q_HEADS * HEAD_DIM
    d_qkv = d_attn + 2 * N_KV_HEADS * HEAD_DIM
    d_rnn = rnn_conv_w.shape[-1]

    h = x.reshape(bsz * seq, d)
    for l in range(depth):
        d_in = w_in.shape[-1]
        rest_of_w_in = (w_in[l], ((d_qkv, d_qkv + 2 * d_rnn, 1.0), (d_qkv + 2 * d_rnn, d_in, 0.5)))
        qkv, hb, w_rxy, w_gl_half = _qkv_proj(h, w_in[l], d_qkv, d_attn, HEAD_DIM ** -0.5 * LOG2E, (rest_of_w_in,))
        gates, w_up = _gates_proj(hb, w_gl_half, b_gate[l], (_whole(ffn_w_up[l]),))
        o, w_gate, w_ap, w_rp = _attention(qkv, attn_sinks[l], bsz, seq,
                                           (_whole(ffn_w_gate[l]), _whole(w_attn_proj[l]), _whole(w_rnn_proj[l])))

        wg = _gate_superblocks(lru_wa[l], lru_wi[l])
        z = _rnn(h, w_rxy, rnn_conv_w[l], rnn_conv_b[l], wg, lru_ba[l], lru_bi[l], lru_lambda[l], bsz, seq)

        m, w_down, w_o = _merge(o, z, gates, w_ap, w_rp, (_whole(ffn_w_down[l]), _whole(w_out[l])))
        h = _out_ln(m, h, w_o, ln1_g[l], ln1_b[l], alpha)
        h = _ffn(h, w_up, w_gate, ffn_conv_w[l], ffn_conv_b[l], w_down, ln2_g[l], ln2_b[l], alpha, seq)
    return h.reshape(bsz, seq, d)
```
